```python
import math
import jax
import jax.numpy as jnp
from jax import lax
import numpy as np

D_MODEL = 1024
BATCH = 16
SEQ = 2048
DEPTH = 1

GRID_W = 64
CTX_LEN = 256

D_SSM = 1024
SSM_HEAD_DIM = 64
SSM_HEADS = D_SSM // SSM_HEAD_DIM
SSM_GROUPS = 2
SSM_STATE = 128
SSM_CONV = 3
SSD_CHUNK = 128

D_HYENA = 1024
HY_ORDER = 2
HY_SHORT = 3
HY_EMB = 33
HY_BANDS = (HY_EMB - 1) // 2
HY_FILTER_WIDTH = 64
HY_DECAY_TARGET = 1e-2
HY_FAST_DECAY = 0.3
HY_SLOW_DECAY = 1.5

N_EXPERTS = 64
TOP_K = 6
N_EXPERT_GROUPS = 8
TOPK_GROUPS = 4
D_EXPERT = 256
D_SHARED = 256
ROUTED_SCALE = 2.5
EXPERT_BLOCK = 128

NORM_EPS = 1e-6
N_ADA = 6

N_XB = D_SSM + SSM_GROUPS * SSM_STATE
N_XBC = D_SSM + 2 * SSM_GROUPS * SSM_STATE
N_DT = 2 * SSM_HEADS
N_Z = D_SSM
N_HY = (HY_ORDER + 1) * D_HYENA
N_GATE = 2 * D_MODEL
N_IN = N_XBC + N_DT + N_Z + N_HY + N_GATE
IN_SPLITS = (N_XBC, N_XBC + N_DT, N_XBC + N_DT + N_Z, N_XBC + N_DT + N_Z + N_HY)

kernel_name = 'hyena_ssd_moe_prefix_dit_block'

F32 = jnp.float32


def rmsnorm(x, g):
    xf = x.astype(F32)
    y = xf * lax.rsqrt(jnp.mean(xf * xf, axis=-1, keepdims=True) + NORM_EPS)
    return (y * g.astype(F32)).astype(x.dtype)


def modulate(h, shift, scale):
    return h * (1 + scale) + shift


def seq_flip(a):
    return jnp.flip(a, axis=1)


def dwconv(u, w, b):
    k = w.shape[0]
    y = lax.conv_general_dilated(u, w[:, None, :].astype(u.dtype), window_strides=(1,),
                                 padding=[(k // 2, k // 2)], dimension_numbers=('NWC', 'WIO', 'NWC'),
                                 feature_group_count=u.shape[-1])
    return y + b.astype(u.dtype)


def gated_rmsnorm(y, z, g):
    v = y.astype(F32) * jax.nn.silu(z.astype(F32))
    shp = v.shape
    v = v.reshape(shp[:-1] + (SSM_GROUPS, shp[-1] // SSM_GROUPS))
    v = v * lax.rsqrt(jnp.mean(v * v, axis=-1, keepdims=True) + NORM_EPS)
    return v.reshape(shp) * g.astype(F32)


def swiglu(x, wg, wu, wd):
    return (jax.nn.silu(x @ wg) * (x @ wu)) @ wd


def ssd_dt(dt_raw, dt_bias):
    b, l, _ = dt_raw.shape
    return jax.nn.softplus(dt_raw.astype(F32).reshape(b, l, 2, SSM_HEADS) + dt_bias.astype(F32))


def _ssd_prep(x, dt, A, B):
    b, l, h, p = x.shape
    g, n = B.shape[-2:]
    c = l // SSD_CHUNK
    xdt = (x.astype(F32) * dt[..., None]).reshape(b, c, SSD_CHUNK, g, h // g, p)
    a_cs = jnp.cumsum((dt * A).reshape(b, c, SSD_CHUNK, g, h // g), axis=2)
    Bc = B.astype(F32).reshape(b, c, SSD_CHUNK, g, n)
    return xdt, a_cs, Bc


def ssd_scan(x, dt, A, B, C, h0):
    b, l, h, p = x.shape
    g, n = B.shape[-2:]
    c = l // SSD_CHUNK
    xdt, a_cs, Bc = _ssd_prep(x, dt, A, B)
    Cc = C.astype(F32).reshape(b, c, SSD_CHUNK, g, n)
    to_end = jnp.exp(a_cs[:, :, -1:] - a_cs)
    chunk_states = jnp.einsum('bcsgn,bcsghp->bcghpn', Bc, xdt * to_end[..., None])
    chunk_decay = jnp.exp(a_cs[:, :, -1])

    def carry_state(state, inp):
        st, dec = inp
        return state * dec[..., None, None] + st, state

    final, s_in = lax.scan(carry_state, h0.astype(F32).reshape(b, g, h // g, p, n),
                           (jnp.moveaxis(chunk_states, 1, 0), jnp.moveaxis(chunk_decay, 1, 0)))
    s_in = jnp.moveaxis(s_in, 0, 1)
    lower = jnp.tril(jnp.ones((SSD_CHUNK, SSD_CHUNK), bool))[:, :, None, None]
    seg = a_cs[:, :, :, None] - a_cs[:, :, None, :]
    decay = jnp.exp(jnp.where(lower, seg, -jnp.inf))
    cb = jnp.einsum('bclgn,bcsgn->bclsg', Cc, Bc)
    y_diag = jnp.einsum('bclsgh,bcsghp->bclghp', cb[..., None] * decay, xdt)
    y_off = jnp.einsum('bclgn,bcghpn->bclghp', Cc, s_in) * jnp.exp(a_cs)[..., None]
    return (y_diag + y_off).reshape(b, l, h, p), final.reshape(b, h, p, n)


def ssd_final_state(x, dt, A, B, h0):
    b, l, h, p = x.shape
    g, n = B.shape[-2:]
    xdt, a_cs, Bc = _ssd_prep(x, dt, A, B)
    la = a_cs[:, :, -1]
    total = jnp.sum(la, axis=1)
    later = total[:, None] - jnp.cumsum(la, axis=1)
    to_end = jnp.exp(a_cs[:, :, -1:] - a_cs + later[:, :, None])
    contrib = jnp.einsum('bcsgn,bcsghp->bghpn', Bc, xdt * to_end[..., None])
    final = jnp.exp(total)[..., None, None] * h0.astype(F32).reshape(b, g, h // g, p, n) + contrib
    return final.reshape(b, h, p, n)


def implicit_filters(L, p):
    pos = jnp.arange(L, dtype=F32)
    t = pos / max(L - 1, 1)
    bands = jnp.linspace(1e-4, HY_BANDS - 1, HY_BANDS, dtype=F32)
    ang = (2.0 * math.pi / L) * pos[:, None] * bands[None, :]
    z = jnp.concatenate([t[:, None], jnp.cos(ang), -jnp.sin(ang)], axis=-1)
    freq = p['hy_freq'].astype(F32)
    hdn = jnp.sin(freq[0] * (z @ p['hy_w1'].astype(F32) + p['hy_b1'].astype(F32)))
    hdn = jnp.sin(freq[1] * (hdn @ p['hy_w2'].astype(F32) + p['hy_b2'].astype(F32)))
    hdn = jnp.sin(freq[2] * (hdn @ p['hy_w3'].astype(F32) + p['hy_b3'].astype(F32)))
    h = (hdn @ p['hy_w_out'].astype(F32)).reshape(L, 2, HY_ORDER, D_HYENA)
    window = jnp.exp(-t[:, None, None] * jnp.abs(p['hy_decay'].astype(F32)))
    return h * window[:, None]


def long_conv(u, h_fwd, h_bwd, skip):
    L = u.shape[1]
    k = jnp.concatenate([h_fwd, jnp.zeros_like(h_fwd[:1]), h_bwd[:0:-1]], axis=0)
    uf = jnp.fft.rfft(u.astype(F32), n=2 * L, axis=1)
    kf = jnp.fft.rfft(k, axis=0)
    y = jnp.fft.irfft(uf * kf[None], n=2 * L, axis=1)[:, :L]
    return (y + u.astype(F32) * skip.astype(F32)).astype(u.dtype)


def hyena(u, p):
    L = u.shape[1]
    u = dwconv(u, p['hy_conv_w'], p['hy_conv_b'])
    parts = jnp.split(u, HY_ORDER + 1, axis=-1)
    z = parts[-1]
    h = implicit_filters(L, p)
    for k in range(HY_ORDER):
        z = parts[k] * long_conv(z, h[:, 0, k], h[:, 1, k], p['hy_bias'][k])
    return z


def mixer(h, p, init_f, init_b):
    b, l, _ = h.shape
    proj = h @ p['w_in']
    xbc, dt_raw, z, hy, gate_logits = jnp.split(proj, IN_SPLITS, axis=-1)
    xbc = jax.nn.silu(dwconv(xbc, p['ssm_conv_w'], p['ssm_conv_b']))
    xs, Bm, Cm = jnp.split(xbc, [D_SSM, N_XB], axis=-1)
    xs = xs.reshape(b, l, SSM_HEADS, SSM_HEAD_DIM)
    Bm = Bm.reshape(b, l, SSM_GROUPS, SSM_STATE)
    Cm = Cm.reshape(b, l, SSM_GROUPS, SSM_STATE)
    dt = ssd_dt(dt_raw, p['ssm_dt_bias'])
    A = -jnp.exp(p['ssm_a_log'].astype(F32))
    y_f, s_f = ssd_scan(xs, dt[:, :, 0], A[0], Bm, Cm, init_f)
    y_b, s_b = ssd_scan(seq_flip(xs), seq_flip(dt[:, :, 1]), A[1], seq_flip(Bm), seq_flip(Cm), init_b)
    y = y_f + seq_flip(y_b) + xs.astype(F32) * p['ssm_d'].astype(F32)[:, None]
    y = gated_rmsnorm(y.reshape(b, l, D_SSM), z, p['ssm_norm_g']).astype(h.dtype)
    ssd_out = y @ p['w_ssm_out']
    hy_out = hyena(hy, p) @ p['w_hy_out']
    g = jax.nn.sigmoid(gate_logits.astype(F32)).astype(h.dtype)
    merged = g[..., :D_MODEL] * hy_out + g[..., D_MODEL:] * ssd_out
    return merged @ p['w_o'], s_f, s_b


def context_states(hc, p, init):
    b, l, _ = hc.shape
    w = jnp.concatenate([p['w_in'][:, :N_XB], p['w_in'][:, N_XBC:N_XBC + N_DT]], axis=1)
    xb, dt_raw = jnp.split(hc @ w, [N_XB], axis=-1)
    xb = jax.nn.silu(dwconv(xb, p['ssm_conv_w'][:, :N_XB], p['ssm_conv_b'][:N_XB]))
    xs = xb[..., :D_SSM].reshape(b, l, SSM_HEADS, SSM_HEAD_DIM)
    Bm = xb[..., D_SSM:].reshape(b, l, SSM_GROUPS, SSM_STATE)
    dt = ssd_dt(dt_raw, p['ssm_dt_bias'])
    A = -jnp.exp(p['ssm_a_log'].astype(F32))
    s_f = ssd_final_state(xs, dt[:, :, 0], A[0], Bm, init)
    s_b = ssd_final_state(seq_flip(xs), seq_flip(dt[:, :, 1]), A[1], seq_flip(Bm), init)
    return s_f, s_b


def route(xf, router_w, router_bias):
    t = xf.shape[0]
    per_group = N_EXPERTS // N_EXPERT_GROUPS
    scores = jax.nn.sigmoid((xf @ router_w).astype(F32))
    sel = scores + router_bias.astype(F32)
    grp_score = jnp.sum(lax.top_k(sel.reshape(t, N_EXPERT_GROUPS, per_group), 2)[0], axis=-1)
    top_groups = lax.top_k(grp_score, TOPK_GROUPS)[1]
    keep = jnp.any(top_groups[:, :, None] == jnp.arange(N_EXPERT_GROUPS)[None, None, :], axis=1)
    sel = jnp.where(jnp.repeat(keep, per_group, axis=1), sel, -jnp.inf)
    idx = lax.top_k(sel, TOP_K)[1]
    w = jnp.take_along_axis(scores, idx, axis=1)
    w = w / jnp.sum(w, axis=-1, keepdims=True) * ROUTED_SCALE
    return idx, w


def routed_experts(xf, idx, w, e_gate, e_up, e_down):
    t, d = xf.shape
    k = idx.shape[1]
    s = t * k
    flat_e = idx.reshape(-1)
    order = jnp.argsort(flat_e)
    sorted_e = flat_e[order]
    counts = jnp.zeros((N_EXPERTS,), jnp.int32).at[flat_e].add(1)
    padded = (counts + EXPERT_BLOCK - 1) // EXPERT_BLOCK * EXPERT_BLOCK
    pad_end = jnp.cumsum(padded)
    pad_start = pad_end - padded
    start = jnp.cumsum(counts) - counts
    dest = pad_start[sorted_e] + jnp.arange(s, dtype=jnp.int32) - start[sorted_e]
    n_rows = -(-s // EXPERT_BLOCK) * EXPERT_BLOCK + N_EXPERTS * EXPERT_BLOCK
    nb = n_rows // EXPERT_BLOCK
    row_tok = jnp.zeros((n_rows,), jnp.int32).at[dest].set((order // k).astype(jnp.int32))
    row_w = jnp.zeros((n_rows,), xf.dtype).at[dest].set(w.reshape(-1)[order].astype(xf.dtype))
    block_e = jnp.minimum(jnp.searchsorted(pad_end, jnp.arange(nb, dtype=jnp.int32) * EXPERT_BLOCK, side='right'),
                          N_EXPERTS - 1)

    def expert_block(acc, blk):
        tok, wt, e = blk
        y = swiglu(xf[tok], e_gate[e], e_up[e], e_down[e])
        return acc.at[tok].add(y * wt[:, None]), None

    out, _ = lax.scan(expert_block, jnp.zeros_like(xf),
                      (row_tok.reshape(nb, EXPERT_BLOCK), row_w.reshape(nb, EXPERT_BLOCK), block_e))
    return out


def moe(h, p):
    b, l, d = h.shape
    xf = h.reshape(b * l, d)
    idx, w = route(xf, p['router_w'], p['router_bias'])
    routed = routed_experts(xf, idx, w, p['e_w_gate'], p['e_w_up'], p['e_w_down'])
    shared = swiglu(xf, p['sh_w_gate'], p['sh_w_up'], p['sh_w_down'])
    return (routed + shared).reshape(b, l, d)


def setup_inputs(seed: int = 0) -> dict:
    key = jax.random.key(seed)
    ks = iter(jax.random.split(key, 48))

    def nrm(shape, scale):
        return jax.random.normal(next(ks), shape, F32) * scale

    nl = DEPTH
    dt0 = jnp.exp(jax.random.uniform(next(ks), (nl, 2, SSM_HEADS), F32, math.log(1e-3), math.log(1e-1)))
    a0 = jax.random.uniform(next(ks), (nl, 2, SSM_HEADS), F32, 1.0, 16.0)
    decay_base = jnp.linspace(math.log(1.0 / HY_DECAY_TARGET) / HY_SLOW_DECAY,
                              math.log(1.0 / HY_DECAY_TARGET) / HY_FAST_DECAY, D_HYENA, dtype=F32)
    return {
        'x': nrm((BATCH, SEQ, D_MODEL), 1.0),
        'c': nrm((BATCH, D_MODEL), 1.0),
        'ctx': nrm((BATCH, CTX_LEN, D_MODEL), 1.0),
        'c_ctx': nrm((D_MODEL,), 1.0),
        'ada_w': nrm((nl, D_MODEL, N_ADA * D_MODEL), 0.5 * D_MODEL ** -0.5),
        'ada_b': nrm((nl, N_ADA * D_MODEL), 0.02),
        'norm1_g': 1.0 + nrm((nl, D_MODEL), 0.02),
        'norm2_g': 1.0 + nrm((nl, D_MODEL), 0.02),
        'w_in': nrm((nl, D_MODEL, N_IN), D_MODEL ** -0.5),
        'ssm_conv_w': nrm((nl, SSM_CONV, N_XBC), SSM_CONV ** -0.5),
        'ssm_conv_b': nrm((nl, N_XBC), 0.02),
        'ssm_dt_bias': dt0 + jnp.log(-jnp.expm1(-dt0)),
        'ssm_a_log': jnp.log(a0),
        'ssm_d': 1.0 + nrm((nl, SSM_HEADS), 0.1),
        'ssm_norm_g': 1.0 + nrm((nl, D_SSM), 0.02),
        'w_ssm_out': nrm((nl, D_SSM, D_MODEL), D_SSM ** -0.5),
        'hy_conv_w': nrm((nl, HY_SHORT, N_HY), HY_SHORT ** -0.5),
        'hy_conv_b': nrm((nl, N_HY), 0.02),
        'hy_w1': nrm((nl, HY_EMB, HY_FILTER_WIDTH), HY_EMB ** -0.5),
        'hy_b1': nrm((nl, HY_FILTER_WIDTH), 0.1),
        'hy_w2': nrm((nl, HY_FILTER_WIDTH, HY_FILTER_WIDTH), HY_FILTER_WIDTH ** -0.5),
        'hy_b2': nrm((nl, HY_FILTER_WIDTH), 0.1),
        'hy_w3': nrm((nl, HY_FILTER_WIDTH, HY_FILTER_WIDTH), HY_FILTER_WIDTH ** -0.5),
        'hy_b3': nrm((nl, HY_FILTER_WIDTH), 0.1),
        'hy_freq': 1.0 + nrm((nl, 3, HY_FILTER_WIDTH), 0.01),
        'hy_w_out': nrm((nl, HY_FILTER_WIDTH, 2 * HY_ORDER * D_HYENA), 0.01),
        'hy_decay': decay_base * (1.0 + nrm((nl, HY_ORDER, D_HYENA), 0.05)),
        'hy_bias': nrm((nl, HY_ORDER, D_HYENA), 0.5),
        'w_hy_out': nrm((nl, D_HYENA, D_MODEL), D_HYENA ** -0.5),
        'w_o': nrm((nl, D_MODEL, D_MODEL), D_MODEL ** -0.5),
        'router_w': nrm((nl, D_MODEL, N_EXPERTS), D_MODEL ** -0.5),
        'router_bias': nrm((nl, N_EXPERTS), 0.01),
        'e_w_gate': nrm((nl, N_EXPERTS, D_MODEL, D_EXPERT), D_MODEL ** -0.5),
        'e_w_up': nrm((nl, N_EXPERTS, D_MODEL, D_EXPERT), D_MODEL ** -0.5),
        'e_w_down': nrm((nl, N_EXPERTS, D_EXPERT, D_MODEL), D_EXPERT ** -0.5),
        'sh_w_gate': nrm((nl, D_MODEL, D_SHARED), D_MODEL ** -0.5),
        'sh_w_up': nrm((nl, D_MODEL, D_SHARED), D_MODEL ** -0.5),
        'sh_w_down': nrm((nl, D_SHARED, D_MODEL), D_SHARED ** -0.5),
        'final_g': 1.0 + nrm((D_MODEL,), 0.02),
    }


def reference(x, c, ctx, c_ctx, ada_w, ada_b, norm1_g, norm2_g, w_in, ssm_conv_w, ssm_conv_b,
              ssm_dt_bias, ssm_a_log, ssm_d, ssm_norm_g, w_ssm_out, hy_conv_w, hy_conv_b,
              hy_w1, hy_b1, hy_w2, hy_b2, hy_w3, hy_b3, hy_freq, hy_w_out, hy_decay, hy_bias,
              w_hy_out, w_o, router_w, router_bias, e_w_gate, e_w_up, e_w_down,
              sh_w_gate, sh_w_up, sh_w_down, final_g):
    bsz = x.shape[0]
    for i in range(DEPTH):
        p = dict(w_in=w_in[i], ssm_conv_w=ssm_conv_w[i], ssm_conv_b=ssm_conv_b[i],
                 ssm_dt_bias=ssm_dt_bias[i], ssm_a_log=ssm_a_log[i], ssm_d=ssm_d[i],
                 ssm_norm_g=ssm_norm_g[i], w_ssm_out=w_ssm_out[i], hy_conv_w=hy_conv_w[i],
                 hy_conv_b=hy_conv_b[i], hy_w1=hy_w1[i], hy_b1=hy_b1[i], hy_w2=hy_w2[i],
                 hy_b2=hy_b2[i], hy_w3=hy_w3[i], hy_b3=hy_b3[i], hy_freq=hy_freq[i],
                 hy_w_out=hy_w_out[i], hy_decay=hy_decay[i], hy_bias=hy_bias[i],
                 w_hy_out=w_hy_out[i], w_o=w_o[i], router_w=router_w[i], router_bias=router_bias[i],
                 e_w_gate=e_w_gate[i], e_w_up=e_w_up[i], e_w_down=e_w_down[i],
                 sh_w_gate=sh_w_gate[i], sh_w_up=sh_w_up[i], sh_w_down=sh_w_down[i])
        mod = (jax.nn.silu(c) @ ada_w[i] + ada_b[i]).reshape(bsz, N_ADA, 1, D_MODEL)
        mod_c = (jax.nn.silu(c_ctx) @ ada_w[i] + ada_b[i]).reshape(N_ADA, D_MODEL)
        hx = modulate(rmsnorm(x, norm1_g[i]), mod[:, 0], mod[:, 1])
        hc = modulate(rmsnorm(ctx, norm1_g[i]), mod_c[0], mod_c[1])
        zero_state = jnp.zeros((bsz, SSM_HEADS, SSM_HEAD_DIM, SSM_STATE), F32)
        if i == DEPTH - 1:
            s_f, s_b = context_states(hc, p, zero_state)
        else:
            ctx_mix, s_f, s_b = mixer(hc, p, zero_state, zero_state)
            ctx = ctx + mod_c[2] * ctx_mix
            ctx = ctx + mod_c[5] * moe(modulate(rmsnorm(ctx, norm2_g[i]), mod_c[3], mod_c[4]), p)
        x_mix, _, _ = mixer(hx, p, s_f, s_b)
        x = x + mod[:, 2] * x_mix
        x = x + mod[:, 5] * moe(modulate(rmsnorm(x, norm2_g[i]), mod[:, 3], mod[:, 4]), p)
    return rmsnorm(x, final_g)
```

```python
import functools
import math

import jax
import jax.numpy as jnp
from jax import lax
from jax.experimental import pallas as pl
from jax.experimental.pallas import tpu as pltpu

F32 = jnp.float32
BF16 = jnp.bfloat16

NORM_EPS = 1e-6
SSM_HEADS = 16
SSM_HEAD_DIM = 64
SSM_GROUPS = 2
SSM_STATE = 128
SSD_CHUNK = 128
HY_ORDER = 2
HY_BANDS = 16
N_EXPERTS = 64
TOP_K = 6
N_EXPERT_GROUPS = 8
TOPK_GROUPS = 4
ROUTED_SCALE = 2.5
EXPERT_BLOCK = 128

LANES = 128
BF16_SUBLANES = 16
VMEM_LIMIT = 56 * 1024 * 1024


def _cparams(sem):
    return pltpu.CompilerParams(dimension_semantics=sem, vmem_limit_bytes=VMEM_LIMIT)


def _silu(x):
    return x * jax.nn.sigmoid(x)


def _split_bf16(q):
    hi = q.astype(BF16)
    lo = (q - hi.astype(F32)).astype(BF16)
    return hi, lo


def _normmod_body(x_ref, g_ref, sh_ref, sc_ref, o_ref):
    x = x_ref[0]
    r = lax.rsqrt(jnp.mean(x * x, axis=-1, keepdims=True) + NORM_EPS)
    y = x * r * g_ref[...]
    o_ref[0] = (y * (1.0 + sc_ref[0]) + sh_ref[0]).astype(o_ref.dtype)


def _normmod(x, g, shift, scale, out_dtype=BF16):
    b, l, d = x.shape
    tl = min(l, 512)
    return pl.pallas_call(
        _normmod_body,
        grid=(b, l // tl),
        in_specs=[pl.BlockSpec((1, tl, d), lambda i, j: (i, j, 0)),
                  pl.BlockSpec((1, d), lambda i, j: (0, 0)),
                  pl.BlockSpec((1, 1, d), lambda i, j: (i, 0, 0)),
                  pl.BlockSpec((1, 1, d), lambda i, j: (i, 0, 0))],
        out_specs=pl.BlockSpec((1, tl, d), lambda i, j: (i, j, 0)),
        out_shape=jax.ShapeDtypeStruct((b, l, d), out_dtype),
        compiler_params=_cparams(("parallel", "parallel")),
        name="normmod",
    )(x, g.reshape(1, d), shift.reshape(b, 1, d), scale.reshape(b, 1, d))


def _mm_body(a_ref, w_ref, o_ref):
    o_ref[...] = jnp.dot(a_ref[...], w_ref[...], preferred_element_type=F32).astype(o_ref.dtype)


def _matmul(a, w, out_dtype, tm, tn, name):
    m, k = a.shape
    n = w.shape[1]
    tm, tn = min(tm, m), min(tn, n)
    assert m % tm == 0 and n % tn == 0, (m, n, tm, tn)
    return pl.pallas_call(
        _mm_body,
        grid=(m // tm, n // tn),
        in_specs=[pl.BlockSpec((tm, k), lambda i, j: (i, 0)),
                  pl.BlockSpec((k, tn), lambda i, j: (0, j))],
        out_specs=pl.BlockSpec((tm, tn), lambda i, j: (i, j)),
        out_shape=jax.ShapeDtypeStruct((m, n), out_dtype),
        compiler_params=_cparams(("parallel", "arbitrary")),
        name=name,
    )(a, w)


def _conv3(u, prev_row, next_row, w_ref, b_ref):
    q = u.shape[0]
    row = lax.broadcasted_iota(jnp.int32, u.shape, 0)
    um1 = jnp.where(row == 0, prev_row, pltpu.roll(u, 1, 0))
    up1 = jnp.where(row == q - 1, next_row, pltpu.roll(u, q - 1, 0))
    return um1 * w_ref[0:1, :] + u * w_ref[1:2, :] + up1 * w_ref[2:3, :] + b_ref[...]


def _ssd_body(nc, emit_y, xs_ref, xsp_ref, xsn_ref, bc_ref, bcp_ref, bcn_ref, dt_ref, z_ref,
              cwx_ref, cbx_ref, cwbc_ref, cbbc_ref, dtb_ref, a_ref, dexp_ref, ng_ref,
              e_ref, e2_ref, init_ref, o_ref, st_ref, ysc_ref):
    q = SSD_CHUNK
    p = pl.program_id(1)
    c = pl.program_id(2)
    is_fwd = p == 1
    ci = jnp.where(is_fwd, c, nc - 1 - c)

    @pl.when(c == 0)
    def _():
        st_ref[...] = init_ref[0, 0]

    keep_prev = (ci > 0).astype(F32)
    keep_next = (ci < nc - 1).astype(F32)
    hrow = BF16_SUBLANES - 1
    xs = _silu(_conv3(xs_ref[0].astype(F32),
                      xsp_ref[0].astype(F32)[hrow:hrow + 1] * keep_prev,
                      xsn_ref[0].astype(F32)[0:1] * keep_next, cwx_ref, cbx_ref))
    bc = _silu(_conv3(bc_ref[0].astype(F32),
                      bcp_ref[0].astype(F32)[hrow:hrow + 1] * keep_prev,
                      bcn_ref[0].astype(F32)[0:1] * keep_next, cwbc_ref, cbbc_ref))
    ng2 = SSM_GROUPS * SSM_STATE
    bm, cm = bc[:, :ng2], bc[:, ng2:]

    dtr = dt_ref[0] + dtb_ref[0]
    dt = jnp.maximum(dtr, 0.0) + jnp.log(1.0 + jnp.exp(-jnp.abs(dtr)))
    a = dt * a_ref[0]

    rr = lax.broadcasted_iota(jnp.int32, (q, q), 0)
    cc = lax.broadcasted_iota(jnp.int32, (q, q), 1)
    mask = (rr - cc) * jnp.where(is_fwd, 1, -1) >= 0
    tri = mask.astype(BF16)
    a_hi, a_lo = _split_bf16(a)
    acs2 = jnp.dot(tri, jnp.concatenate([a_hi, a_lo], axis=1), preferred_element_type=F32)
    acs = acs2[:, :LANES] + acs2[:, LANES:]
    c_hi, c_lo = _split_bf16(acs)
    acs_cat = jnp.concatenate([c_hi, c_lo], axis=1)
    ex = jnp.dot(acs_cat, e_ref[...], preferred_element_type=F32)
    d_hi, d_lo = _split_bf16(dt)
    dt_exp = jnp.dot(jnp.concatenate([d_hi, d_lo], axis=1), e_ref[...], preferred_element_type=F32)
    ex_end = jnp.where(is_fwd, ex[q - 1:q, :], ex[0:1, :])

    xdt = xs * dt_exp
    xw = (xdt * jnp.exp(ex_end - ex)).astype(BF16)
    state = st_ref[...]

    if emit_y:
        acs_t = acs.T
        cb_all = jnp.dot(acs_cat, e2_ref[...], preferred_element_type=F32)
        lane = lax.broadcasted_iota(jnp.int32, (q, LANES), 1)
        state_bf = state.astype(BF16)
        hpg = SSM_HEADS // SSM_GROUPS
        gw = hpg * SSM_HEAD_DIM
        pieces = []
        for g in range(SSM_GROUPS):
            cg = cm[:, g * SSM_STATE:(g + 1) * SSM_STATE].astype(BF16)
            bg = bm[:, g * SSM_STATE:(g + 1) * SSM_STATE].astype(BF16)
            cbg = lax.dot_general(cg, bg, (((1,), (1,)), ((), ())), preferred_element_type=F32)
            yoff = jnp.dot(cg, state_bf[:, g * gw:(g + 1) * gw], preferred_element_type=F32)
            yoff = yoff * jnp.exp(ex[:, g * gw:(g + 1) * gw])
            for j in range(hpg // 2):
                h0 = g * hpg + 2 * j
                ms = []
                for h in (h0, h0 + 1):
                    seg = cb_all[:, h * q:(h + 1) * q] - acs_t[h:h + 1, :]
                    ms.append((cbg * jnp.exp(jnp.where(mask, seg, -jnp.inf))).astype(BF16))
                m2 = jnp.concatenate(ms, axis=1)
                xp = xdt[:, h0 * SSM_HEAD_DIM:(h0 + 2) * SSM_HEAD_DIM]
                x2 = jnp.concatenate([jnp.where(lane < SSM_HEAD_DIM, xp, 0.0),
                                      jnp.where(lane >= SSM_HEAD_DIM, xp, 0.0)], axis=0).astype(BF16)
                yd = jnp.dot(m2, x2, preferred_element_type=F32)
                pieces.append(yd + yoff[:, 2 * j * SSM_HEAD_DIM:(2 * j + 2) * SSM_HEAD_DIM])
        y = jnp.concatenate(pieces, axis=1)
        row0 = pl.multiple_of(ci * q, q)

        @pl.when(p == 0)
        def _():
            ysc_ref[pl.ds(row0, q), :] = y

        @pl.when(p == 1)
        def _():
            yt = y + ysc_ref[pl.ds(row0, q), :] + xs * dexp_ref[...]
            v = yt * _silu(z_ref[0].astype(F32))
            outs = []
            for g in range(SSM_GROUPS):
                vg = v[:, g * gw:(g + 1) * gw]
                outs.append(vg * lax.rsqrt(jnp.mean(vg * vg, axis=-1, keepdims=True) + NORM_EPS))
            o_ref[0] = (jnp.concatenate(outs, axis=1) * ng_ref[...]).astype(o_ref.dtype)

    new_parts = []
    hpg = SSM_HEADS // SSM_GROUPS
    gw = hpg * SSM_HEAD_DIM
    for g in range(SSM_GROUPS):
        bt = bm[:, g * SSM_STATE:(g + 1) * SSM_STATE].T.astype(BF16)
        new_parts.append(jnp.dot(bt, xw[:, g * gw:(g + 1) * gw], preferred_element_type=F32))
    st_new = state * jnp.exp(ex_end) + jnp.concatenate(new_parts, axis=1)
    st_ref[...] = st_new
    if not emit_y:
        @pl.when(c == nc - 1)
        def _():
            o_ref[0, 0] = st_new


def _ssd(proj, dt_raw, init, prm, emit_y, xs_col, bc_col, z_col):
    b, l, _ = proj.shape
    q = SSD_CHUNK
    nc = l // q
    hb = q // BF16_SUBLANES
    nh = l // BF16_SUBLANES
    dx = SSM_HEADS * SSM_HEAD_DIM
    dbc = 2 * SSM_GROUPS * SSM_STATE

    def ci_of(p, c):
        return p * c + (1 - p) * (nc - 1 - c)

    def main(col):
        return lambda i, p, c: (i, ci_of(p, c), col)

    def prev(col):
        return lambda i, p, c: (i, jnp.maximum(ci_of(p, c) * hb - 1, 0), col)

    def nxt(col):
        return lambda i, p, c: (i, jnp.minimum(ci_of(p, c) * hb + hb, nh - 1), col)

    const2 = lambda i, p, c: (0, 0)
    bydir = lambda i, p, c: (1 - p, 0, 0)
    in_specs = [
        pl.BlockSpec((1, q, dx), main(xs_col)),
        pl.BlockSpec((1, BF16_SUBLANES, dx), prev(xs_col)),
        pl.BlockSpec((1, BF16_SUBLANES, dx), nxt(xs_col)),
        pl.BlockSpec((1, q, dbc), main(bc_col)),
        pl.BlockSpec((1, BF16_SUBLANES, dbc), prev(bc_col)),
        pl.BlockSpec((1, BF16_SUBLANES, dbc), nxt(bc_col)),
        pl.BlockSpec((1, q, LANES), lambda i, p, c: (i, ci_of(p, c), 1 - p)),
        pl.BlockSpec((1, q, dx), (lambda i, p, c: (i, c * p, z_col))),
        pl.BlockSpec((3, dx), const2), pl.BlockSpec((1, dx), const2),
        pl.BlockSpec((3, dbc), const2), pl.BlockSpec((1, dbc), const2),
        pl.BlockSpec((1, 1, LANES), bydir), pl.BlockSpec((1, 1, LANES), bydir),
        pl.BlockSpec((1, dx), const2), pl.BlockSpec((1, dx), const2),
        pl.BlockSpec((2 * LANES, dx), const2),
        pl.BlockSpec((2 * LANES, SSM_HEADS * q), const2),
        pl.BlockSpec((1, 1, SSM_STATE, dx), lambda i, p, c: (i, 1 - p, 0, 0)),
    ]
    if emit_y:
        out_spec = pl.BlockSpec((1, q, dx), lambda i, p, c: (i, c * p, 0))
        out_shape = jax.ShapeDtypeStruct((b, l, dx), BF16)
    else:
        out_spec = pl.BlockSpec((1, 1, SSM_STATE, dx), lambda i, p, c: (i, 1 - p, 0, 0))
        out_shape = jax.ShapeDtypeStruct((b, 2, SSM_STATE, dx), F32)
    return pl.pallas_call(
        functools.partial(_ssd_body, nc, emit_y),
        grid=(b, 2, nc),
        in_specs=in_specs,
        out_specs=out_spec,
        out_shape=out_shape,
        scratch_shapes=[pltpu.VMEM((SSM_STATE, dx), F32),
                        pltpu.VMEM((l if emit_y else q, dx), F32)],
        compiler_params=_cparams(("parallel", "arbitrary", "arbitrary")),
        name="ssd_scan" if emit_y else "ssd_ctx_state",
    )(proj, proj, proj, proj, proj, proj, dt_raw, proj,
      prm["cwx"], prm["cbx"], prm["cwbc"], prm["cbbc"], prm["dtb"], prm["a"], prm["dexp"], prm["ng"],
      prm["e"], prm["e2"], init)


def _hyena_body(nfb, fb, x1_ref, x2_ref, v_ref, cw1_ref, cw2_ref, cw3_ref, cb1_ref, cb2_ref, cb3_ref,
                wf_ref, wi_ref, coef_ref, skip_ref, o_ref, zc_ref, zf_ref, acc_ref, x1c_ref, x2c_ref):
    o = pl.program_id(2)
    k = pl.program_id(3)

    @pl.when((o == 0) & (k == 0))
    def _():
        zero = jnp.zeros((1, x1c_ref.shape[1]), F32)
        x1c_ref[...] = _conv3(x1_ref[0].astype(F32), zero, zero, cw1_ref, cb1_ref)
        x2c_ref[...] = _conv3(x2_ref[0].astype(F32), zero, zero, cw2_ref, cb2_ref)
        vc = _conv3(v_ref[0].astype(F32), zero, zero, cw3_ref, cb3_ref)
        zf_ref[...] = vc
        zc_ref[...] = vc.astype(BF16)

    @pl.when(k == 0)
    def _():
        acc_ref[...] = jnp.zeros_like(acc_ref)

    u = jnp.dot(wf_ref[...], zc_ref[...], preferred_element_type=F32)
    ure, uim = u[:fb], u[fb:]
    yre = ure * coef_ref[0, 0] - uim * coef_ref[0, 1]
    yim = ure * coef_ref[0, 2] + uim * coef_ref[0, 3]
    y = jnp.concatenate([yre, yim], axis=0).astype(BF16)
    acc_ref[...] += jnp.dot(wi_ref[...], y, preferred_element_type=F32)

    @pl.when(k == nfb - 1)
    def _():
        skip = jnp.where(o == 0, skip_ref[0:1, :], skip_ref[1:2, :])
        lc = acc_ref[...] + zf_ref[...] * skip

        @pl.when(o == 0)
        def _():
            zn = x1c_ref[...] * lc
            zf_ref[...] = zn
            zc_ref[...] = zn.astype(BF16)

        @pl.when(o == 1)
        def _():
            o_ref[0] = (x2c_ref[...] * lc).astype(o_ref.dtype)


def _hyena(proj, cw, cb, wf, wi, coef, skip, ct, fb):
    b, l, _ = proj.shape
    d = skip.shape[1]
    nct = d // ct
    nfb = l // fb

    def col(part):
        return lambda j, i, o, k: (i, 0, part * nct + j)

    def wcol(part):
        return lambda j, i, o, k: (0, part * nct + j)

    return pl.pallas_call(
        functools.partial(_hyena_body, nfb, fb),
        grid=(nct, b, HY_ORDER, nfb),
        in_specs=[pl.BlockSpec((1, l, ct), col(0)), pl.BlockSpec((1, l, ct), col(1)),
                  pl.BlockSpec((1, l, ct), col(2)),
                  pl.BlockSpec((3, ct), wcol(0)), pl.BlockSpec((3, ct), wcol(1)), pl.BlockSpec((3, ct), wcol(2)),
                  pl.BlockSpec((1, ct), wcol(0)), pl.BlockSpec((1, ct), wcol(1)), pl.BlockSpec((1, ct), wcol(2)),
                  pl.BlockSpec((2 * fb, l), lambda j, i, o, k: (k, 0)),
                  pl.BlockSpec((l, 2 * fb), lambda j, i, o, k: (0, k)),
                  pl.BlockSpec((1, 4, fb, ct), lambda j, i, o, k: (o, 0, k, j)),
                  pl.BlockSpec((HY_ORDER, ct), lambda j, i, o, k: (0, j))],
        out_specs=pl.BlockSpec((1, l, ct), lambda j, i, o, k: (i, 0, j)),
        out_shape=jax.ShapeDtypeStruct((b, l, d), BF16),
        scratch_shapes=[pltpu.VMEM((l, ct), BF16), pltpu.VMEM((l, ct), F32), pltpu.VMEM((l, ct), F32),
                        pltpu.VMEM((l, ct), F32), pltpu.VMEM((l, ct), F32)],
        compiler_params=_cparams(("parallel", "parallel", "arbitrary", "arbitrary")),
        name="hyena",
    )(proj, proj, proj, cw, cw, cw, cb, cb, cb, wf, wi, coef, skip)


def _dft_tables(l, fb):
    n = 2 * l
    f = jnp.arange(l, dtype=jnp.int32)
    t = jnp.arange(l, dtype=jnp.int32)
    ang = (2.0 * math.pi / n) * ((f[:, None] * t[None, :]) % n).astype(F32)
    alt = jnp.where(t % 2 == 0, 1.0, -1.0).astype(F32)
    cos, sin = jnp.cos(ang), jnp.sin(ang)
    f_re = cos
    f_im = jnp.where(f[:, None] == 0, alt[None, :], -sin)
    cf = jnp.where(f[:, None] == 0, 1.0 / n, 2.0 / n).astype(F32)
    i_re = cf * cos
    i_im = jnp.where(f[:, None] == 0, alt[None, :] / n, -(2.0 / n) * sin)

    def blocked(re, im):
        return jnp.concatenate([re.reshape(l // fb, 1, fb, l), im.reshape(l // fb, 1, fb, l)], axis=1).reshape(n, l)

    wf = blocked(f_re, f_im).astype(BF16)
    wi = blocked(i_re, i_im).T.astype(BF16)
    return wf, wi


def _implicit_filters(l, hp):
    hi = lax.Precision.HIGHEST
    pos = jnp.arange(l, dtype=F32)
    t = pos / max(l - 1, 1)
    bands = jnp.linspace(1e-4, HY_BANDS - 1, HY_BANDS, dtype=F32)
    ang = (2.0 * math.pi / l) * pos[:, None] * bands[None, :]
    z = jnp.concatenate([t[:, None], jnp.cos(ang), -jnp.sin(ang)], axis=-1)
    freq = hp["hy_freq"]
    hdn = jnp.sin(freq[0] * (jnp.dot(z, hp["hy_w1"], precision=hi) + hp["hy_b1"]))
    hdn = jnp.sin(freq[1] * (jnp.dot(hdn, hp["hy_w2"], precision=hi) + hp["hy_b2"]))
    hdn = jnp.sin(freq[2] * (jnp.dot(hdn, hp["hy_w3"], precision=hi) + hp["hy_b3"]))
    d = hp["hy_decay"].shape[-1]
    h = jnp.dot(hdn, hp["hy_w_out"], precision=hi).reshape(l, 2, HY_ORDER, d)
    window = jnp.exp(-t[:, None, None] * jnp.abs(hp["hy_decay"]))
    return h * window[:, None]


def _filter_coefs(l, fb, wf, hp):
    h = _implicit_filters(l, hp)
    d = h.shape[-1]
    hf = h[:, 0]
    hb = h[:, 1].at[0].set(0.0)
    hcat = jnp.concatenate([hf.reshape(l, HY_ORDER * d), hb.reshape(l, HY_ORDER * d)], axis=1).astype(BF16)
    s = _matmul(wf, hcat, F32, 512, 512, "filter_dft")
    s = s.reshape(l // fb, 2, fb, 2, HY_ORDER, d)
    s = jnp.moveaxis(s, 1, 0).reshape(2, l, 2, HY_ORDER, d)
    kre = s[0, :, 0] + s[0, :, 1]
    kim = s[1, :, 0] - s[1, :, 1]
    knyq = s[1, 0, 0] + s[1, 0, 1]
    c1 = kim.at[0].set(0.0)
    c3 = kre.at[0].set(knyq)
    coef = jnp.stack([kre, c1, c1, c3], axis=0)
    return jnp.moveaxis(coef, 2, 0)


def _outproj_body(zhy_ref, yn_ref, ghy_ref, gssd_ref, x_ref, g1_ref, sh2_ref, sc2_ref, n2g_ref,
                  why_ref, wssm_ref, wo_ref, x1_ref, h2_ref):
    hyo = jnp.dot(zhy_ref[0], why_ref[...], preferred_element_type=F32)
    sso = jnp.dot(yn_ref[0], wssm_ref[...], preferred_element_type=F32)
    merged = (jax.nn.sigmoid(ghy_ref[0].astype(F32)) * hyo
              + jax.nn.sigmoid(gssd_ref[0].astype(F32)) * sso)
    xmix = jnp.dot(merged.astype(BF16), wo_ref[...], preferred_element_type=F32)
    x1 = x_ref[0] + g1_ref[0] * xmix
    x1_ref[0] = x1
    r = lax.rsqrt(jnp.mean(x1 * x1, axis=-1, keepdims=True) + NORM_EPS)
    h2_ref[0] = (x1 * r * n2g_ref[...]) * (1.0 + sc2_ref[0]) + sh2_ref[0]


def _outproj(zhy, yn, proj, x, gate1, shift2, scale2, n2g, why, wssm, wo, ghy_col, gssd_col):
    b, l, d = x.shape
    tm = min(l, 512)
    tile = lambda col: pl.BlockSpec((1, tm, d), lambda i, j: (i, j, col))
    row = pl.BlockSpec((1, 1, d), lambda i, j: (i, 0, 0))
    wspec = pl.BlockSpec((d, d), lambda i, j: (0, 0))
    return pl.pallas_call(
        _outproj_body,
        grid=(b, l // tm),
        in_specs=[tile(0), tile(0), tile(ghy_col), tile(gssd_col), tile(0), row, row, row,
                  pl.BlockSpec((1, d), lambda i, j: (0, 0)), wspec, wspec, wspec],
        out_specs=[tile(0), tile(0)],
        out_shape=[jax.ShapeDtypeStruct((b, l, d), F32), jax.ShapeDtypeStruct((b, l, d), F32)],
        compiler_params=_cparams(("parallel", "parallel")),
        name="merge_outproj",
    )(zhy, yn, proj, proj, x, gate1.reshape(b, 1, d), shift2.reshape(b, 1, d), scale2.reshape(b, 1, d),
      n2g.reshape(1, d), why, wssm, wo)


def _router_body(h_ref, wh_ref, wl_ref, o_ref):
    x_hi, x_lo = _split_bf16(h_ref[...])
    logits = (jnp.dot(x_hi, wh_ref[...], preferred_element_type=F32)
              + jnp.dot(x_hi, wl_ref[...], preferred_element_type=F32)
              + jnp.dot(x_lo, wh_ref[...], preferred_element_type=F32))
    o_ref[...] = jax.nn.sigmoid(logits)


def _router_scores(h2, router_w):
    t, d = h2.shape
    wpad = jnp.zeros((d, LANES), F32).at[:, :N_EXPERTS].set(router_w)
    w_hi, w_lo = _split_bf16(wpad)
    tm = min(t, 512)
    return pl.pallas_call(
        _router_body,
        grid=(t // tm,),
        in_specs=[pl.BlockSpec((tm, d), lambda i: (i, 0)),
                  pl.BlockSpec((d, LANES), lambda i: (0, 0)),
                  pl.BlockSpec((d, LANES), lambda i: (0, 0))],
        out_specs=pl.BlockSpec((tm, LANES), lambda i: (i, 0)),
        out_shape=jax.ShapeDtypeStruct((t, LANES), F32),
        compiler_params=_cparams(("parallel",)),
        name="router",
    )(h2, w_hi, w_lo)[:, :N_EXPERTS]


def _route(scores, router_bias):
    t = scores.shape[0]
    per_group = N_EXPERTS // N_EXPERT_GROUPS
    sel = scores + router_bias.astype(F32)
    grp_score = jnp.sum(lax.top_k(sel.reshape(t, N_EXPERT_GROUPS, per_group), 2)[0], axis=-1)
    top_groups = lax.top_k(grp_score, TOPK_GROUPS)[1]
    keep = jnp.any(top_groups[:, :, None] == jnp.arange(N_EXPERT_GROUPS)[None, None, :], axis=1)
    sel = jnp.where(jnp.repeat(keep, per_group, axis=1), sel, -jnp.inf)
    idx = lax.top_k(sel, TOP_K)[1]
    w = jnp.take_along_axis(scores, idx, axis=1)
    w = w / jnp.sum(w, axis=-1, keepdims=True) * ROUTED_SCALE
    return idx, w


def _dispatch_plan(idx):
    t, k = idx.shape
    s = t * k
    flat_e = idx.reshape(-1)
    order = jnp.argsort(flat_e)
    sorted_e = flat_e[order]
    counts = jnp.zeros((N_EXPERTS,), jnp.int32).at[flat_e].add(1)
    padded = (counts + EXPERT_BLOCK - 1) // EXPERT_BLOCK * EXPERT_BLOCK
    pad_end = jnp.cumsum(padded)
    pad_start = pad_end - padded
    start = jnp.cumsum(counts) - counts
    dest = pad_start[sorted_e] + jnp.arange(s, dtype=jnp.int32) - start[sorted_e]
    n_rows = -(-s // EXPERT_BLOCK) * EXPERT_BLOCK + N_EXPERTS * EXPERT_BLOCK
    nb = n_rows // EXPERT_BLOCK
    row_tok = jnp.zeros((n_rows,), jnp.int32).at[dest].set((order // k).astype(jnp.int32))
    block_e = jnp.minimum(jnp.searchsorted(pad_end, jnp.arange(nb, dtype=jnp.int32) * EXPERT_BLOCK, side="right"),
                          N_EXPERTS - 1).astype(jnp.int32)
    pos = jnp.zeros((s,), jnp.int32).at[order].set(dest.astype(jnp.int32)).reshape(t, k)
    return row_tok.reshape(nb, 1, EXPERT_BLOCK), block_e, pos


def _experts_body(be_ref, tok_ref, tokn_ref, h_hbm, wg_ref, wu_ref, wd_ref, o_ref, buf_ref, sem_ref):
    i = pl.program_id(0)
    nb = pl.num_programs(0)
    slot = lax.rem(i, 2)
    rb = EXPERT_BLOCK

    def row_copy(tok, r, s):
        return pltpu.make_async_copy(h_hbm.at[pl.ds(tok, 1)], buf_ref.at[s, pl.ds(r, 1)], sem_ref.at[s])

    def issue(idx_ref, s):
        def body(r, carry):
            row_copy(idx_ref[0, 0, r], r, s).start()
            return carry
        lax.fori_loop(0, rb, body, 0)

    @pl.when(i == 0)
    def _():
        issue(tok_ref, 0)

    @pl.when(i + 1 < nb)
    def _():
        issue(tokn_ref, 1 - slot)

    pltpu.make_async_copy(h_hbm.at[pl.ds(0, rb)], buf_ref.at[slot], sem_ref.at[slot]).wait()

    x = buf_ref[slot].astype(BF16)
    hg = jnp.dot(x, wg_ref[0], preferred_element_type=F32)
    hu = jnp.dot(x, wu_ref[0], preferred_element_type=F32)
    act = (_silu(hg) * hu).astype(BF16)
    o_ref[...] = jnp.dot(act, wd_ref[0], preferred_element_type=F32)


def _experts(h2, row_tok, block_e, wg, wu, wd):
    nb = row_tok.shape[0]
    t, d = h2.shape
    de = wg.shape[2]
    rb = EXPERT_BLOCK
    grid_spec = pltpu.PrefetchScalarGridSpec(
        num_scalar_prefetch=1,
        grid=(nb,),
        in_specs=[pl.BlockSpec((1, 1, rb), lambda i, be: (i, 0, 0), memory_space=pltpu.SMEM),
                  pl.BlockSpec((1, 1, rb), lambda i, be: (jnp.minimum(i + 1, nb - 1), 0, 0),
                               memory_space=pltpu.SMEM),
                  pl.BlockSpec(memory_space=pl.ANY),
                  pl.BlockSpec((1, d, de), lambda i, be: (be[i], 0, 0)),
                  pl.BlockSpec((1, d, de), lambda i, be: (be[i], 0, 0)),
                  pl.BlockSpec((1, de, d), lambda i, be: (be[i], 0, 0))],
        out_specs=pl.BlockSpec((rb, d), lambda i, be: (i, 0)),
        scratch_shapes=[pltpu.VMEM((2, rb, d), F32), pltpu.SemaphoreType.DMA((2,))],
    )
    return pl.pallas_call(
        _experts_body,
        grid_spec=grid_spec,
        out_shape=jax.ShapeDtypeStruct((nb * rb, d), F32),
        compiler_params=_cparams(("arbitrary",)),
        name="experts",
    )(block_e, row_tok, row_tok, h2, wg, wu, wd)


def _combine_body(tm, pos_ref, posn_ref, ys_hbm, w_ref, h_ref, x1_ref, g2_ref, fg_ref,
                  sg_ref, su_ref, sd_ref, o_ref, buf_ref, sem_ref):
    i = pl.program_id(0)
    nt = pl.num_programs(0)
    slot = lax.rem(i, 2)
    nrow = TOP_K * tm

    def row_copy(src, r, s):
        return pltpu.make_async_copy(ys_hbm.at[pl.ds(src, 1)], buf_ref.at[s, pl.ds(r, 1)], sem_ref.at[s])

    def issue(idx_ref, s):
        def body(r, carry):
            row_copy(idx_ref[0, 0, r], r, s).start()
            return carry
        lax.fori_loop(0, nrow, body, 0)

    @pl.when(i == 0)
    def _():
        issue(pos_ref, 0)

    @pl.when(i + 1 < nt)
    def _():
        issue(posn_ref, 1 - slot)

    pltpu.make_async_copy(ys_hbm.at[pl.ds(0, nrow)], buf_ref.at[slot], sem_ref.at[slot]).wait()

    w = w_ref[...]
    routed = jnp.zeros(h_ref.shape, F32)
    for k in range(TOP_K):
        routed = routed + w[:, k:k + 1] * buf_ref[slot, k * tm:(k + 1) * tm, :]
    xb = h_ref[...].astype(BF16)
    hg = jnp.dot(xb, sg_ref[...], preferred_element_type=F32)
    hu = jnp.dot(xb, su_ref[...], preferred_element_type=F32)
    shared = jnp.dot((_silu(hg) * hu).astype(BF16), sd_ref[...], preferred_element_type=F32)
    x2 = x1_ref[...] + g2_ref[0] * (routed + shared)
    r = lax.rsqrt(jnp.mean(x2 * x2, axis=-1, keepdims=True) + NORM_EPS)
    o_ref[...] = x2 * r * fg_ref[...]


def _combine(ys, pos, wts, h2, x1, gate2, final_g, sg, su, sd, tokens_per_batch):
    t, d = h2.shape
    tm = min(tokens_per_batch, 128)
    nt = t // tm
    tpb = tokens_per_batch // tm
    ds = sg.shape[1]
    pos_t = pos.reshape(nt, tm, TOP_K).transpose(0, 2, 1).reshape(nt, 1, TOP_K * tm)
    wpad = jnp.zeros((t, LANES), F32).at[:, :TOP_K].set(wts)
    tile = pl.BlockSpec((tm, d), lambda i: (i, 0))
    return pl.pallas_call(
        functools.partial(_combine_body, tm),
        grid=(nt,),
        in_specs=[pl.BlockSpec((1, 1, TOP_K * tm), lambda i: (i, 0, 0), memory_space=pltpu.SMEM),
                  pl.BlockSpec((1, 1, TOP_K * tm), lambda i: (jnp.minimum(i + 1, nt - 1), 0, 0),
                               memory_space=pltpu.SMEM),
                  pl.BlockSpec(memory_space=pl.ANY),
                  pl.BlockSpec((tm, LANES), lambda i: (i, 0)),
                  tile, tile,
                  pl.BlockSpec((1, 1, d), lambda i: (i // tpb, 0, 0)),
                  pl.BlockSpec((1, d), lambda i: (0, 0)),
                  pl.BlockSpec((d, ds), lambda i: (0, 0)),
                  pl.BlockSpec((d, ds), lambda i: (0, 0)),
                  pl.BlockSpec((ds, d), lambda i: (0, 0))],
        out_specs=tile,
        out_shape=jax.ShapeDtypeStruct((t, d), F32),
        scratch_shapes=[pltpu.VMEM((2, TOP_K * tm, d), F32), pltpu.SemaphoreType.DMA((2,))],
        compiler_params=_cparams(("arbitrary",)),
        name="moe_combine",
    )(pos_t, pos_t, ys, wpad, h2, x1, gate2, final_g.reshape(1, d), sg, su, sd)


def _ssd_params(ssm_conv_w, ssm_conv_b, ssm_dt_bias, ssm_a_log, ssm_d, ssm_norm_g):
    dx = SSM_HEADS * SSM_HEAD_DIM
    q = SSD_CHUNK
    pad = lambda v: jnp.zeros((2, 1, LANES), F32).at[:, 0, :SSM_HEADS].set(v)
    heads = jnp.arange(SSM_HEADS)
    e1 = (jnp.arange(dx)[None, :] // SSM_HEAD_DIM == heads[:, None]).astype(F32)
    e2 = (jnp.arange(SSM_HEADS * q)[None, :] // q == heads[:, None]).astype(F32)

    def stack(e):
        z = jnp.zeros((2 * LANES, e.shape[1]), F32)
        return z.at[:SSM_HEADS].set(e).at[LANES:LANES + SSM_HEADS].set(e).astype(BF16)

    return dict(
        cwx=ssm_conv_w[:, :dx], cbx=ssm_conv_b[None, :dx],
        cwbc=ssm_conv_w[:, dx:], cbbc=ssm_conv_b[None, dx:],
        dtb=pad(ssm_dt_bias), a=pad(-jnp.exp(ssm_a_log.astype(F32))),
        dexp=jnp.repeat(ssm_d, SSM_HEAD_DIM)[None, :], ng=ssm_norm_g[None, :],
        e=stack(e1), e2=stack(e2))


def kernel(x, c, ctx, c_ctx, ada_w, ada_b, norm1_g, norm2_g, w_in, ssm_conv_w, ssm_conv_b, ssm_dt_bias, ssm_a_log, ssm_d, ssm_norm_g, w_ssm_out, hy_conv_w, hy_conv_b, hy_w1, hy_b1, hy_w2, hy_b2, hy_w3, hy_b3, hy_freq, hy_w_out, hy_decay, hy_bias, w_hy_out, w_o, router_w, router_bias, e_w_gate, e_w_up, e_w_down, sh_w_gate, sh_w_up, sh_w_down, final_g):
    depth = ada_w.shape[0]
    assert depth == 1, "single-layer block"
    b, l, d = x.shape
    lc = ctx.shape[1]
    dx = SSM_HEADS * SSM_HEAD_DIM
    dbc = 2 * SSM_GROUPS * SSM_STATE
    assert dx == d and hy_bias.shape[-1] == d

    rows = -(-(b + 1) // BF16_SUBLANES) * BF16_SUBLANES
    cc = jnp.zeros((rows, d), F32).at[:b].set(c).at[b].set(c_ctx)
    mod_all = _matmul(_silu(cc).astype(BF16), ada_w[0].astype(BF16), F32, rows, 512, "adaln") + ada_b[0]
    mod = mod_all[:b].reshape(b, 6, d)
    mod_c = jnp.broadcast_to(mod_all[b].reshape(1, 6, d), (b, 6, d))

    w = w_in[0]
    o_dt, o_z, o_hy, o_g = dx + dbc, dx + dbc + 2 * SSM_HEADS, 2 * dx + dbc + 2 * SSM_HEADS, 2 * dx + dbc + 2 * SSM_HEADS + 3 * d
    w_main = jnp.concatenate([w[:, o_hy:o_g], w[:, o_g:], w[:, o_z:o_hy], w[:, :dx + dbc]], axis=1).astype(BF16)
    w_dt = (jnp.zeros((d, 2 * LANES), F32)
            .at[:, :SSM_HEADS].set(w[:, o_dt:o_dt + SSM_HEADS])
            .at[:, LANES:LANES + SSM_HEADS].set(w[:, o_dt + SSM_HEADS:o_z])).astype(BF16)
    col_ghy, col_gssd, col_z, col_xs, col_bc = 3, 4, 5, 6, 7 * (d // dbc)

    sp = _ssd_params(ssm_conv_w[0], ssm_conv_b[0], ssm_dt_bias[0], ssm_a_log[0], ssm_d[0], ssm_norm_g[0])

    hc = _normmod(ctx, norm1_g[0], mod_c[:, 0], mod_c[:, 1])
    w_ctx = w_main[:, col_xs * d:]
    proj_c = _matmul(hc.reshape(b * lc, d), w_ctx, BF16, 1024, 512, "ctx_proj").reshape(b, lc, -1)
    dt_c = _matmul(hc.reshape(b * lc, d), w_dt, F32, 1024, 2 * LANES, "ctx_dt").reshape(b, lc, 2 * LANES)
    zero_state = jnp.zeros((b, 2, SSM_STATE, dx), F32)
    ctx_state = _ssd(proj_c, dt_c, zero_state, sp, False, 0, d // dbc, 0)

    hx = _normmod(x, norm1_g[0], mod[:, 0], mod[:, 1]).reshape(b * l, d)
    proj = _matmul(hx, w_main, BF16, 1024, 512, "in_proj").reshape(b, l, -1)
    dt_raw = _matmul(hx, w_dt, F32, 1024, 2 * LANES, "dt_proj").reshape(b, l, 2 * LANES)
    yn = _ssd(proj, dt_raw, ctx_state, sp, True, col_xs, col_bc, col_z)

    fb = min(l, 512)
    ct = 256
    wf, wi = _dft_tables(l, fb)
    hp = dict(hy_w1=hy_w1[0], hy_b1=hy_b1[0], hy_w2=hy_w2[0], hy_b2=hy_b2[0], hy_w3=hy_w3[0], hy_b3=hy_b3[0],
              hy_freq=hy_freq[0], hy_w_out=hy_w_out[0], hy_decay=hy_decay[0])
    coef = _filter_coefs(l, fb, wf, hp)
    zhy = _hyena(proj, hy_conv_w[0], hy_conv_b[0][None, :], wf, wi, coef, hy_bias[0], ct, fb)

    x1, h2 = _outproj(zhy, yn, proj, x, mod[:, 2], mod[:, 3], mod[:, 4], norm2_g[0],
                      w_hy_out[0].astype(BF16), w_ssm_out[0].astype(BF16), w_o[0].astype(BF16),
                      col_ghy, col_gssd)
    x1 = x1.reshape(b * l, d)
    h2 = h2.reshape(b * l, d)

    scores = _router_scores(h2, router_w[0])
    idx, wts = _route(scores, router_bias[0])
    row_tok, block_e, pos = _dispatch_plan(idx)
    ys = _experts(h2, row_tok, block_e, e_w_gate[0].astype(BF16), e_w_up[0].astype(BF16),
                  e_w_down[0].astype(BF16))
    out = _combine(ys, pos, wts, h2, x1, mod[:, 5].reshape(b, 1, d), final_g,
                   sh_w_gate[0].astype(BF16), sh_w_up[0].astype(BF16), sh_w_down[0].astype(BF16), l)
    return out.reshape(b, l, d)
```

```python
import functools
import math

import jax
import jax.numpy as jnp
from jax import lax
from jax.experimental import pallas as pl
from jax.experimental.pallas import tpu as pltpu

F32 = jnp.float32
BF16 = jnp.bfloat16

NORM_EPS = 1e-6
SSM_HEADS = 16
SSM_HEAD_DIM = 64
SSM_GROUPS = 2
SSM_STATE = 128
SSD_CHUNK = 128
HY_ORDER = 2
HY_BANDS = 16
N_EXPERTS = 64
TOP_K = 6
N_EXPERT_GROUPS = 8
TOPK_GROUPS = 4
ROUTED_SCALE = 2.5
EXPERT_BLOCK = 128

LANES = 128
BF16_SUBLANES = 16
VMEM_LIMIT = 56 * 1024 * 1024


def _cparams(sem):
    return pltpu.CompilerParams(dimension_semantics=sem, vmem_limit_bytes=VMEM_LIMIT)


def _silu(x):
    return x * jax.nn.sigmoid(x)


def _split_bf16(q):
    hi = q.astype(BF16)
    lo = (q - hi.astype(F32)).astype(BF16)
    return hi, lo


def _normmod_body(x_ref, g_ref, sh_ref, sc_ref, o_ref):
    x = x_ref[0]
    r = lax.rsqrt(jnp.mean(x * x, axis=-1, keepdims=True) + NORM_EPS)
    y = x * r * g_ref[...]
    o_ref[0] = (y * (1.0 + sc_ref[0]) + sh_ref[0]).astype(o_ref.dtype)


def _normmod(x, g, shift, scale, out_dtype=BF16):
    b, l, d = x.shape
    tl = min(l, 512)
    return pl.pallas_call(
        _normmod_body,
        grid=(b, l // tl),
        in_specs=[pl.BlockSpec((1, tl, d), lambda i, j: (i, j, 0)),
                  pl.BlockSpec((1, d), lambda i, j: (0, 0)),
                  pl.BlockSpec((1, 1, d), lambda i, j: (i, 0, 0)),
                  pl.BlockSpec((1, 1, d), lambda i, j: (i, 0, 0))],
        out_specs=pl.BlockSpec((1, tl, d), lambda i, j: (i, j, 0)),
        out_shape=jax.ShapeDtypeStruct((b, l, d), out_dtype),
        compiler_params=_cparams(("parallel", "parallel")),
        name="normmod",
    )(x, g.reshape(1, d), shift.reshape(b, 1, d), scale.reshape(b, 1, d))


def _mm_body(a_ref, w_ref, o_ref):
    o_ref[...] = jnp.dot(a_ref[...], w_ref[...], preferred_element_type=F32).astype(o_ref.dtype)


def _matmul(a, w, out_dtype, tm, tn, name):
    m, k = a.shape
    n = w.shape[1]
    tm, tn = min(tm, m), min(tn, n)
    assert m % tm == 0 and n % tn == 0, (m, n, tm, tn)
    return pl.pallas_call(
        _mm_body,
        grid=(m // tm, n // tn),
        in_specs=[pl.BlockSpec((tm, k), lambda i, j: (i, 0)),
                  pl.BlockSpec((k, tn), lambda i, j: (0, j))],
        out_specs=pl.BlockSpec((tm, tn), lambda i, j: (i, j)),
        out_shape=jax.ShapeDtypeStruct((m, n), out_dtype),
        compiler_params=_cparams(("parallel", "arbitrary")),
        name=name,
    )(a, w)


def _conv3(u, prev_row, next_row, w_ref, b_ref):
    q = u.shape[0]
    row = lax.broadcasted_iota(jnp.int32, u.shape, 0)
    um1 = jnp.where(row == 0, prev_row, pltpu.roll(u, 1, 0))
    up1 = jnp.where(row == q - 1, next_row, pltpu.roll(u, q - 1, 0))
    return um1 * w_ref[0:1, :] + u * w_ref[1:2, :] + up1 * w_ref[2:3, :] + b_ref[...]


def _ssd_body(nc, emit_y, xs_ref, xsp_ref, xsn_ref, bc_ref, bcp_ref, bcn_ref, dt_ref, z_ref,
              cwx_ref, cbx_ref, cwbc_ref, cbbc_ref, dtb_ref, a_ref, dexp_ref, ng_ref,
              e_ref, e2_ref, init_ref, o_ref, st_ref, ysc_ref):
    q = SSD_CHUNK
    p = pl.program_id(1)
    c = pl.program_id(2)
    is_fwd = p == 1
    ci = jnp.where(is_fwd, c, nc - 1 - c)

    @pl.when(c == 0)
    def _():
        st_ref[...] = init_ref[0, 0]

    keep_prev = (ci > 0).astype(F32)
    keep_next = (ci < nc - 1).astype(F32)
    hrow = BF16_SUBLANES - 1
    xs = _silu(_conv3(xs_ref[0].astype(F32),
                      xsp_ref[0].astype(F32)[hrow:hrow + 1] * keep_prev,
                      xsn_ref[0].astype(F32)[0:1] * keep_next, cwx_ref, cbx_ref))
    bc = _silu(_conv3(bc_ref[0].astype(F32),
                      bcp_ref[0].astype(F32)[hrow:hrow + 1] * keep_prev,
                      bcn_ref[0].astype(F32)[0:1] * keep_next, cwbc_ref, cbbc_ref))
    ng2 = SSM_GROUPS * SSM_STATE
    bm, cm = bc[:, :ng2], bc[:, ng2:]

    dtr = dt_ref[0] + dtb_ref[0]
    dt = jnp.maximum(dtr, 0.0) + jnp.log(1.0 + jnp.exp(-jnp.abs(dtr)))
    a = dt * a_ref[0]

    rr = lax.broadcasted_iota(jnp.int32, (q, q), 0)
    cc = lax.broadcasted_iota(jnp.int32, (q, q), 1)
    mask = (rr - cc) * jnp.where(is_fwd, 1, -1) >= 0
    tri = mask.astype(BF16)
    a_hi, a_lo = _split_bf16(a)
    acs2 = jnp.dot(tri, jnp.concatenate([a_hi, a_lo], axis=1), preferred_element_type=F32)
    acs = acs2[:, :LANES] + acs2[:, LANES:]
    c_hi, c_lo = _split_bf16(acs)
    acs_cat = jnp.concatenate([c_hi, c_lo], axis=1)
    ex = jnp.dot(acs_cat, e_ref[...], preferred_element_type=F32)
    d_hi, d_lo = _split_bf16(dt)
    dt_exp = jnp.dot(jnp.concatenate([d_hi, d_lo], axis=1), e_ref[...], preferred_element_type=F32)
    ex_end = jnp.where(is_fwd, ex[q - 1:q, :], ex[0:1, :])

    xdt = xs * dt_exp
    xw = (xdt * jnp.exp(ex_end - ex)).astype(BF16)
    state = st_ref[...]

    if emit_y:
        acs_t = acs.T
        cb_all = jnp.dot(acs_cat, e2_ref[...], preferred_element_type=F32)
        lane = lax.broadcasted_iota(jnp.int32, (q, LANES), 1)
        state_bf = state.astype(BF16)
        hpg = SSM_HEADS // SSM_GROUPS
        gw = hpg * SSM_HEAD_DIM
        pieces = []
        for g in range(SSM_GROUPS):
            cg = cm[:, g * SSM_STATE:(g + 1) * SSM_STATE].astype(BF16)
            bg = bm[:, g * SSM_STATE:(g + 1) * SSM_STATE].astype(BF16)
            cbg = lax.dot_general(cg, bg, (((1,), (1,)), ((), ())), preferred_element_type=F32)
            yoff = jnp.dot(cg, state_bf[:, g * gw:(g + 1) * gw], preferred_element_type=F32)
            yoff = yoff * jnp.exp(ex[:, g * gw:(g + 1) * gw])
            for j in range(hpg // 2):
                h0 = g * hpg + 2 * j
                ms = []
                for h in (h0, h0 + 1):
                    seg = cb_all[:, h * q:(h + 1) * q] - acs_t[h:h + 1, :]
                    ms.append((cbg * jnp.exp(jnp.where(mask, seg, -jnp.inf))).astype(BF16))
                m2 = jnp.concatenate(ms, axis=1)
                xp = xdt[:, h0 * SSM_HEAD_DIM:(h0 + 2) * SSM_HEAD_DIM]
                x2 = jnp.concatenate([jnp.where(lane < SSM_HEAD_DIM, xp, 0.0),
                                      jnp.where(lane >= SSM_HEAD_DIM, xp, 0.0)], axis=0).astype(BF16)
                yd = jnp.dot(m2, x2, preferred_element_type=F32)
                pieces.append(yd + yoff[:, 2 * j * SSM_HEAD_DIM:(2 * j + 2) * SSM_HEAD_DIM])
        y = jnp.concatenate(pieces, axis=1)
        row0 = pl.multiple_of(ci * q, q)

        @pl.when(p == 0)
        def _():
            ysc_ref[pl.ds(row0, q), :] = y

        @pl.when(p == 1)
        def _():
            yt = y + ysc_ref[pl.ds(row0, q), :] + xs * dexp_ref[...]
            v = yt * _silu(z_ref[0].astype(F32))
            outs = []
            for g in range(SSM_GROUPS):
                vg = v[:, g * gw:(g + 1) * gw]
                outs.append(vg * lax.rsqrt(jnp.mean(vg * vg, axis=-1, keepdims=True) + NORM_EPS))
            o_ref[0] = (jnp.concatenate(outs, axis=1) * ng_ref[...]).astype(o_ref.dtype)

    new_parts = []
    hpg = SSM_HEADS // SSM_GROUPS
    gw = hpg * SSM_HEAD_DIM
    for g in range(SSM_GROUPS):
        bt = bm[:, g * SSM_STATE:(g + 1) * SSM_STATE].T.astype(BF16)
        new_parts.append(jnp.dot(bt, xw[:, g * gw:(g + 1) * gw], preferred_element_type=F32))
    st_new = state * jnp.exp(ex_end) + jnp.concatenate(new_parts, axis=1)
    st_ref[...] = st_new
    if not emit_y:
        @pl.when(c == nc - 1)
        def _():
            o_ref[0, 0] = st_new


def _ssd(proj, dt_raw, init, prm, emit_y, xs_col, bc_col, z_col):
    b, l, _ = proj.shape
    q = SSD_CHUNK
    nc = l // q
    hb = q // BF16_SUBLANES
    nh = l // BF16_SUBLANES
    dx = SSM_HEADS * SSM_HEAD_DIM
    dbc = 2 * SSM_GROUPS * SSM_STATE

    def ci_of(p, c):
        return p * c + (1 - p) * (nc - 1 - c)

    def main(col):
        return lambda i, p, c: (i, ci_of(p, c), col)

    def prev(col):
        return lambda i, p, c: (i, jnp.maximum(ci_of(p, c) * hb - 1, 0), col)

    def nxt(col):
        return lambda i, p, c: (i, jnp.minimum(ci_of(p, c) * hb + hb, nh - 1), col)

    const2 = lambda i, p, c: (0, 0)
    bydir = lambda i, p, c: (1 - p, 0, 0)
    in_specs = [
        pl.BlockSpec((1, q, dx), main(xs_col)),
        pl.BlockSpec((1, BF16_SUBLANES, dx), prev(xs_col)),
        pl.BlockSpec((1, BF16_SUBLANES, dx), nxt(xs_col)),
        pl.BlockSpec((1, q, dbc), main(bc_col)),
        pl.BlockSpec((1, BF16_SUBLANES, dbc), prev(bc_col)),
        pl.BlockSpec((1, BF16_SUBLANES, dbc), nxt(bc_col)),
        pl.BlockSpec((1, q, LANES), lambda i, p, c: (i, ci_of(p, c), 1 - p)),
        pl.BlockSpec((1, q, dx), (lambda i, p, c: (i, c * p, z_col))),
        pl.BlockSpec((3, dx), const2), pl.BlockSpec((1, dx), const2),
        pl.BlockSpec((3, dbc), const2), pl.BlockSpec((1, dbc), const2),
        pl.BlockSpec((1, 1, LANES), bydir), pl.BlockSpec((1, 1, LANES), bydir),
        pl.BlockSpec((1, dx), const2), pl.BlockSpec((1, dx), const2),
        pl.BlockSpec((2 * LANES, dx), const2),
        pl.BlockSpec((2 * LANES, SSM_HEADS * q), const2),
        pl.BlockSpec((1, 1, SSM_STATE, dx), lambda i, p, c: (i, 1 - p, 0, 0)),
    ]
    if emit_y:
        out_spec = pl.BlockSpec((1, q, dx), lambda i, p, c: (i, c * p, 0))
        out_shape = jax.ShapeDtypeStruct((b, l, dx), BF16)
    else:
        out_spec = pl.BlockSpec((1, 1, SSM_STATE, dx), lambda i, p, c: (i, 1 - p, 0, 0))
        out_shape = jax.ShapeDtypeStruct((b, 2, SSM_STATE, dx), F32)
    return pl.pallas_call(
        functools.partial(_ssd_body, nc, emit_y),
        grid=(b, 2, nc),
        in_specs=in_specs,
        out_specs=out_spec,
        out_shape=out_shape,
        scratch_shapes=[pltpu.VMEM((SSM_STATE, dx), F32),
                        pltpu.VMEM((l if emit_y else q, dx), F32)],
        compiler_params=_cparams(("parallel", "arbitrary", "arbitrary")),
        name="ssd_scan" if emit_y else "ssd_ctx_state",
    )(proj, proj, proj, proj, proj, proj, dt_raw, proj,
      prm["cwx"], prm["cbx"], prm["cwbc"], prm["cbbc"], prm["dtb"], prm["a"], prm["dexp"], prm["ng"],
      prm["e"], prm["e2"], init)


def _hyena_body(nfb, fb, x1_ref, x2_ref, v_ref, cw1_ref, cw2_ref, cw3_ref, cb1_ref, cb2_ref, cb3_ref,
                wf_ref, wi_ref, coef_ref, skip_ref, o_ref, zc_ref, zf_ref, acc_ref, x1c_ref, x2c_ref):
    o = pl.program_id(2)
    k = pl.program_id(3)

    @pl.when((o == 0) & (k == 0))
    def _():
        zero = jnp.zeros((1, x1c_ref.shape[1]), F32)
        x1c_ref[...] = _conv3(x1_ref[0].astype(F32), zero, zero, cw1_ref, cb1_ref)
        x2c_ref[...] = _conv3(x2_ref[0].astype(F32), zero, zero, cw2_ref, cb2_ref)
        vc = _conv3(v_ref[0].astype(F32), zero, zero, cw3_ref, cb3_ref)
        zf_ref[...] = vc
        zc_ref[...] = vc.astype(BF16)

    @pl.when(k == 0)
    def _():
        acc_ref[...] = jnp.zeros_like(acc_ref)

    u = jnp.dot(wf_ref[...], zc_ref[...], preferred_element_type=F32)
    ure, uim = u[:fb], u[fb:]
    yre = ure * coef_ref[0, 0] - uim * coef_ref[0, 1]
    yim = ure * coef_ref[0, 2] + uim * coef_ref[0, 3]
    y = jnp.concatenate([yre, yim], axis=0).astype(BF16)
    acc_ref[...] += jnp.dot(wi_ref[...], y, preferred_element_type=F32)

    @pl.when(k == nfb - 1)
    def _():
        skip = jnp.where(o == 0, skip_ref[0:1, :], skip_ref[1:2, :])
        lc = acc_ref[...] + zf_ref[...] * skip

        @pl.when(o == 0)
        def _():
            zn = x1c_ref[...] * lc
            zf_ref[...] = zn
            zc_ref[...] = zn.astype(BF16)

        @pl.when(o == 1)
        def _():
            o_ref[0] = (x2c_ref[...] * lc).astype(o_ref.dtype)


def _hyena(proj, cw, cb, wf, wi, coef, skip, ct, fb):
    b, l, _ = proj.shape
    d = skip.shape[1]
    nct = d // ct
    nfb = l // fb

    def col(part):
        return lambda j, i, o, k: (i, 0, part * nct + j)

    def wcol(part):
        return lambda j, i, o, k: (0, part * nct + j)

    return pl.pallas_call(
        functools.partial(_hyena_body, nfb, fb),
        grid=(nct, b, HY_ORDER, nfb),
        in_specs=[pl.BlockSpec((1, l, ct), col(0)), pl.BlockSpec((1, l, ct), col(1)),
                  pl.BlockSpec((1, l, ct), col(2)),
                  pl.BlockSpec((3, ct), wcol(0)), pl.BlockSpec((3, ct), wcol(1)), pl.BlockSpec((3, ct), wcol(2)),
                  pl.BlockSpec((1, ct), wcol(0)), pl.BlockSpec((1, ct), wcol(1)), pl.BlockSpec((1, ct), wcol(2)),
                  pl.BlockSpec((2 * fb, l), lambda j, i, o, k: (k, 0)),
                  pl.BlockSpec((l, 2 * fb), lambda j, i, o, k: (0, k)),
                  pl.BlockSpec((1, 4, fb, ct), lambda j, i, o, k: (o, 0, k, j)),
                  pl.BlockSpec((HY_ORDER, ct), lambda j, i, o, k: (0, j))],
        out_specs=pl.BlockSpec((1, l, ct), lambda j, i, o, k: (i, 0, j)),
        out_shape=jax.ShapeDtypeStruct((b, l, d), BF16),
        scratch_shapes=[pltpu.VMEM((l, ct), BF16), pltpu.VMEM((l, ct), F32), pltpu.VMEM((l, ct), F32),
                        pltpu.VMEM((l, ct), F32), pltpu.VMEM((l, ct), F32)],
        compiler_params=_cparams(("parallel", "parallel", "arbitrary", "arbitrary")),
        name="hyena",
    )(proj, proj, proj, cw, cw, cw, cb, cb, cb, wf, wi, coef, skip)


def _dft_tables(l, fb):
    n = 2 * l
    f = jnp.arange(l, dtype=jnp.int32)
    t = jnp.arange(l, dtype=jnp.int32)
    ang = (2.0 * math.pi / n) * ((f[:, None] * t[None, :]) % n).astype(F32)
    alt = jnp.where(t % 2 == 0, 1.0, -1.0).astype(F32)
    cos, sin = jnp.cos(ang), jnp.sin(ang)
    f_re = cos
    f_im = jnp.where(f[:, None] == 0, alt[None, :], -sin)
    cf = jnp.where(f[:, None] == 0, 1.0 / n, 2.0 / n).astype(F32)
    i_re = cf * cos
    i_im = jnp.where(f[:, None] == 0, alt[None, :] / n, -(2.0 / n) * sin)

    def blocked(re, im):
        return jnp.concatenate([re.reshape(l // fb, 1, fb, l), im.reshape(l // fb, 1, fb, l)], axis=1).reshape(n, l)

    wf = blocked(f_re, f_im).astype(BF16)
    wi = blocked(i_re, i_im).T.astype(BF16)
    return wf, wi


def _implicit_filters(l, hp):
    hi = lax.Precision.HIGHEST
    pos = jnp.arange(l, dtype=F32)
    t = pos / max(l - 1, 1)
    bands = jnp.linspace(1e-4, HY_BANDS - 1, HY_BANDS, dtype=F32)
    ang = (2.0 * math.pi / l) * pos[:, None] * bands[None, :]
    z = jnp.concatenate([t[:, None], jnp.cos(ang), -jnp.sin(ang)], axis=-1)
    freq = hp["hy_freq"]
    hdn = jnp.sin(freq[0] * (jnp.dot(z, hp["hy_w1"], precision=hi) + hp["hy_b1"]))
    hdn = jnp.sin(freq[1] * (jnp.dot(hdn, hp["hy_w2"], precision=hi) + hp["hy_b2"]))
    hdn = jnp.sin(freq[2] * (jnp.dot(hdn, hp["hy_w3"], precision=hi) + hp["hy_b3"]))
    d = hp["hy_decay"].shape[-1]
    h = jnp.dot(hdn, hp["hy_w_out"], precision=hi).reshape(l, 2, HY_ORDER, d)
    window = jnp.exp(-t[:, None, None] * jnp.abs(hp["hy_decay"]))
    return h * window[:, None]


def _filter_coefs(l, fb, wf, hp):
    h = _implicit_filters(l, hp)
    d = h.shape[-1]
    hf = h[:, 0]
    hb = h[:, 1].at[0].set(0.0)
    hcat = jnp.concatenate([hf.reshape(l, HY_ORDER * d), hb.reshape(l, HY_ORDER * d)], axis=1).astype(BF16)
    s = _matmul(wf, hcat, F32, 512, 512, "filter_dft")
    s = s.reshape(l // fb, 2, fb, 2, HY_ORDER, d)
    s = jnp.moveaxis(s, 1, 0).reshape(2, l, 2, HY_ORDER, d)
    kre = s[0, :, 0] + s[0, :, 1]
    kim = s[1, :, 0] - s[1, :, 1]
    knyq = s[1, 0, 0] + s[1, 0, 1]
    c1 = kim.at[0].set(0.0)
    c3 = kre.at[0].set(knyq)
    coef = jnp.stack([kre, c1, c1, c3], axis=0)
    return jnp.moveaxis(coef, 2, 0)


def _outproj_body(zhy_ref, yn_ref, ghy_ref, gssd_ref, x_ref, g1_ref, sh2_ref, sc2_ref, n2g_ref,
                  why_ref, wssm_ref, wo_ref, x1_ref, h2_ref):
    hyo = jnp.dot(zhy_ref[0], why_ref[...], preferred_element_type=F32)
    sso = jnp.dot(yn_ref[0], wssm_ref[...], preferred_element_type=F32)
    merged = (jax.nn.sigmoid(ghy_ref[0].astype(F32)) * hyo
              + jax.nn.sigmoid(gssd_ref[0].astype(F32)) * sso)
    xmix = jnp.dot(merged.astype(BF16), wo_ref[...], preferred_element_type=F32)
    x1 = x_ref[0] + g1_ref[0] * xmix
    x1_ref[0] = x1
    r = lax.rsqrt(jnp.mean(x1 * x1, axis=-1, keepdims=True) + NORM_EPS)
    h2_ref[0] = (x1 * r * n2g_ref[...]) * (1.0 + sc2_ref[0]) + sh2_ref[0]


def _outproj(zhy, yn, proj, x, gate1, shift2, scale2, n2g, why, wssm, wo, ghy_col, gssd_col):
    b, l, d = x.shape
    tm = min(l, 512)
    tile = lambda col: pl.BlockSpec((1, tm, d), lambda i, j: (i, j, col))
    row = pl.BlockSpec((1, 1, d), lambda i, j: (i, 0, 0))
    wspec = pl.BlockSpec((d, d), lambda i, j: (0, 0))
    return pl.pallas_call(
        _outproj_body,
        grid=(b, l // tm),
        in_specs=[tile(0), tile(0), tile(ghy_col), tile(gssd_col), tile(0), row, row, row,
                  pl.BlockSpec((1, d), lambda i, j: (0, 0)), wspec, wspec, wspec],
        out_specs=[tile(0), tile(0)],
        out_shape=[jax.ShapeDtypeStruct((b, l, d), F32), jax.ShapeDtypeStruct((b, l, d), F32)],
        compiler_params=_cparams(("parallel", "parallel")),
        name="merge_outproj",
    )(zhy, yn, proj, proj, x, gate1.reshape(b, 1, d), shift2.reshape(b, 1, d), scale2.reshape(b, 1, d),
      n2g.reshape(1, d), why, wssm, wo)


PER_GROUP = N_EXPERTS // N_EXPERT_GROUPS


def _route_body(tm, h_ref, wh_ref, wl_ref, bias_ref, e_ref, c_ref, w_ref, cnt_ref, carry_ref):
    ng, pg = N_EXPERT_GROUPS, PER_GROUP

    @pl.when(pl.program_id(0) == 0)
    def _():
        carry_ref[...] = jnp.zeros_like(carry_ref)

    x_hi, x_lo = _split_bf16(h_ref[...])
    nt = (((1,), (1,)), ((), ()))
    lg = (lax.dot_general(wh_ref[...], x_hi, nt, preferred_element_type=F32)
          + lax.dot_general(wh_ref[...], x_lo, nt, preferred_element_type=F32)
          + lax.dot_general(wl_ref[...], x_hi, nt, preferred_element_type=F32))
    sc = jax.nn.sigmoid(lg)
    sel = sc + bias_ref[...]

    mem = [sel[m * ng:(m + 1) * ng, :] for m in range(pg)]
    m1 = functools.reduce(jnp.maximum, mem)
    first = functools.reduce(jnp.minimum, [jnp.where(mem[m] == m1, m, pg) for m in range(pg)])
    m2 = functools.reduce(jnp.maximum, [jnp.where(first == m, -jnp.inf, mem[m]) for m in range(pg)])
    gs = m1 + m2
    gid = lax.broadcasted_iota(jnp.int32, gs.shape, 0)
    beats = jnp.zeros(gs.shape, jnp.int32)
    for g2 in range(ng):
        v = gs[g2:g2 + 1, :]
        beats = beats + ((v > gs) | ((v == gs) & (g2 < gid))).astype(jnp.int32)
    keep = beats < TOPK_GROUPS
    selm = jnp.concatenate([jnp.where(keep, mem[m], -jnp.inf) for m in range(pg)], axis=0)

    r = lax.broadcasted_iota(jnp.int32, selm.shape, 0)
    eid = (r % ng) * pg + r // ng
    rank = jnp.zeros(selm.shape, jnp.int32)
    for r2 in range(N_EXPERTS):
        v = selm[r2:r2 + 1, :]
        e2 = (r2 % ng) * pg + r2 // ng
        rank = rank + ((v > selm) | ((v == selm) & (e2 < eid))).astype(jnp.int32)
    chosen = rank < TOP_K
    wsel = jnp.where(chosen, sc, 0.0)
    wn = wsel / jnp.sum(wsel, axis=0, keepdims=True) * ROUTED_SCALE

    cb = chosen.astype(BF16)
    t0 = lax.broadcasted_iota(jnp.int32, (tm, tm), 0)
    t1 = lax.broadcasted_iota(jnp.int32, (tm, tm), 1)
    before = jnp.dot(cb, (t0 < t1).astype(BF16), preferred_element_type=F32) + carry_ref[...]
    carry = carry_ref[...] + jnp.dot(cb, jnp.ones((tm, tm), BF16), preferred_element_type=F32)
    carry_ref[...] = carry
    cnt_ref[...] = carry

    k8 = lax.broadcasted_iota(jnp.int32, e_ref.shape, 0)
    eidf = eid.astype(F32)
    e_out = jnp.zeros(e_ref.shape, F32)
    c_out = jnp.zeros(e_ref.shape, F32)
    w_out = jnp.zeros(e_ref.shape, F32)
    for k in range(TOP_K):
        mk = rank == k
        e_out = jnp.where(k8 == k, jnp.sum(jnp.where(mk, eidf, 0.0), axis=0, keepdims=True), e_out)
        c_out = jnp.where(k8 == k, jnp.sum(jnp.where(mk, before, 0.0), axis=0, keepdims=True), c_out)
        w_out = jnp.where(k8 == k, jnp.sum(jnp.where(mk, wn, 0.0), axis=0, keepdims=True), w_out)
    e_ref[...] = e_out.astype(jnp.int32)
    c_ref[...] = c_out.astype(jnp.int32)
    w_ref[...] = w_out


def _route(h2, router_w, router_bias):
    t, d = h2.shape
    tm = min(t, 256)
    ng, pg = N_EXPERT_GROUPS, PER_GROUP
    wt = router_w.T.reshape(ng, pg, d).transpose(1, 0, 2).reshape(N_EXPERTS, d)
    w_hi, w_lo = _split_bf16(wt)
    bias = jnp.broadcast_to(router_bias.astype(F32).reshape(ng, pg).T.reshape(N_EXPERTS, 1), (N_EXPERTS, tm))
    slot = pl.BlockSpec((8, tm), lambda i: (0, i))
    const = lambda shape: pl.BlockSpec(shape, lambda i: (0, 0))
    e8, c8, w8, cnt = pl.pallas_call(
        functools.partial(_route_body, tm),
        grid=(t // tm,),
        in_specs=[pl.BlockSpec((tm, d), lambda i: (i, 0)), const((N_EXPERTS, d)), const((N_EXPERTS, d)),
                  const((N_EXPERTS, tm))],
        out_specs=[slot, slot, slot, const((N_EXPERTS, tm))],
        out_shape=[jax.ShapeDtypeStruct((8, t), jnp.int32), jax.ShapeDtypeStruct((8, t), jnp.int32),
                   jax.ShapeDtypeStruct((8, t), F32), jax.ShapeDtypeStruct((N_EXPERTS, tm), F32)],
        scratch_shapes=[pltpu.VMEM((N_EXPERTS, tm), F32)],
        compiler_params=_cparams(("arbitrary",)),
        name="moe_route",
    )(h2, w_hi, w_lo, bias)
    counts = cnt[:, 0].reshape(pg, ng).T.reshape(N_EXPERTS).astype(jnp.int32)
    return e8[:TOP_K], c8[:TOP_K], w8[:TOP_K], counts


def _dispatch_plan(e6, c6, counts, n_slots):
    padded = (counts + EXPERT_BLOCK - 1) // EXPERT_BLOCK * EXPERT_BLOCK
    pad_end = jnp.cumsum(padded)
    pad_start = pad_end - padded
    nb = -(-n_slots // EXPERT_BLOCK) + N_EXPERTS
    blk = jnp.arange(nb, dtype=jnp.int32) * EXPERT_BLOCK
    block_e = jnp.minimum(jnp.sum((pad_end[None, :] <= blk[:, None]).astype(jnp.int32), axis=1), N_EXPERTS - 1)
    n_used = (pad_end[-1] // EXPERT_BLOCK).astype(jnp.int32).reshape(1)
    onehot = e6[..., None] == jnp.arange(N_EXPERTS, dtype=jnp.int32)
    pos = jnp.sum(jnp.where(onehot, pad_start, 0), axis=-1) + c6
    return pos.astype(jnp.int32), block_e.astype(jnp.int32), n_used, (pad_start + counts).astype(jnp.int32), \
        (padded - counts).astype(jnp.int32), nb


def _dispatch_body(tm, nb, zs_ref, zn_ref, nu_ref, pos_ref, h_hbm, xs_hbm, zero_ref, sem_ref, zsem_ref):
    i = pl.program_id(0)
    nt = pl.num_programs(0)
    slot = lax.rem(i, 2)
    nrow = TOP_K * tm

    def body(r, carry):
        for k in range(TOP_K):
            pltpu.make_async_copy(h_hbm.at[pl.ds(i * tm + r, 1)],
                                  xs_hbm.at[pl.ds(pos_ref[0, 0, k * tm + r], 1)], sem_ref.at[slot]).start()
        return carry
    lax.fori_loop(0, tm, body, 0)

    def wait_batch(s):
        pltpu.make_async_copy(h_hbm.at[pl.ds(0, nrow)], xs_hbm.at[pl.ds(0, nrow)], sem_ref.at[s]).wait()

    @pl.when(i > 0)
    def _():
        wait_batch(1 - slot)

    @pl.when(i == nt - 1)
    def _():
        wait_batch(slot)
        zero_ref[...] = jnp.zeros_like(zero_ref)

        def pad_copy(row):
            return pltpu.make_async_copy(zero_ref.at[pl.ds(0, 1)], xs_hbm.at[pl.ds(row, 1)], zsem_ref.at[0])

        def tail_copy(blk):
            row = pl.multiple_of(blk * EXPERT_BLOCK, EXPERT_BLOCK)
            return pltpu.make_async_copy(zero_ref, xs_hbm.at[pl.ds(row, EXPERT_BLOCK)], zsem_ref.at[1])

        def tail_start(blk, carry):
            tail_copy(blk).start()
            return carry
        lax.fori_loop(nu_ref[0], nb, tail_start, 0)

        def tail_wait(blk, carry):
            tail_copy(blk).wait()
            return carry
        lax.fori_loop(nu_ref[0], nb, tail_wait, 0)

        def per_expert(e, carry):
            def start(j, c2):
                pad_copy(zs_ref[e] + j).start()
                return c2
            lax.fori_loop(0, zn_ref[e], start, 0)

            def wait(j, c2):
                pad_copy(zs_ref[e] + j).wait()
                return c2
            lax.fori_loop(0, zn_ref[e], wait, 0)
            return carry
        lax.fori_loop(0, N_EXPERTS, per_expert, 0)


def _dispatch(h2, pos_t, zero_start, zero_n, n_used, nb, tm):
    t, d = h2.shape
    grid_spec = pltpu.PrefetchScalarGridSpec(
        num_scalar_prefetch=3,
        grid=(t // tm,),
        in_specs=[pl.BlockSpec((1, 1, TOP_K * tm), lambda i, zs, zn, nu: (i, 0, 0), memory_space=pltpu.SMEM),
                  pl.BlockSpec(memory_space=pl.ANY)],
        out_specs=pl.BlockSpec(memory_space=pl.ANY),
        scratch_shapes=[pltpu.VMEM((EXPERT_BLOCK, d), F32), pltpu.SemaphoreType.DMA((2,)),
                        pltpu.SemaphoreType.DMA((2,))],
    )
    return pl.pallas_call(
        functools.partial(_dispatch_body, tm, nb),
        grid_spec=grid_spec,
        out_shape=jax.ShapeDtypeStruct((nb * EXPERT_BLOCK, d), F32),
        compiler_params=_cparams(("arbitrary",)),
        name="moe_dispatch",
    )(zero_start, zero_n, n_used, pos_t, h2)


def _experts_body(be_ref, nu_ref, x_ref, wg_ref, wu_ref, wd_ref, o_ref):
    used = pl.program_id(0) < nu_ref[0]

    @pl.when(used)
    def _():
        x = x_ref[...].astype(BF16)
        hg = jnp.dot(x, wg_ref[0], preferred_element_type=F32)
        hu = jnp.dot(x, wu_ref[0], preferred_element_type=F32)
        act = (_silu(hg) * hu).astype(BF16)
        o_ref[...] = jnp.dot(act, wd_ref[0], preferred_element_type=F32)

    @pl.when(jnp.logical_not(used))
    def _():
        o_ref[...] = jnp.zeros_like(o_ref)


def _experts(xs, block_e, n_used, wg, wu, wd):
    n_rows, d = xs.shape
    de = wg.shape[2]
    rb = EXPERT_BLOCK
    last = lambda i, nu: jnp.minimum(i, nu[0] - 1)
    grid_spec = pltpu.PrefetchScalarGridSpec(
        num_scalar_prefetch=2,
        grid=(n_rows // rb,),
        in_specs=[pl.BlockSpec((rb, d), lambda i, be, nu: (last(i, nu), 0)),
                  pl.BlockSpec((1, d, de), lambda i, be, nu: (be[last(i, nu)], 0, 0)),
                  pl.BlockSpec((1, d, de), lambda i, be, nu: (be[last(i, nu)], 0, 0)),
                  pl.BlockSpec((1, de, d), lambda i, be, nu: (be[last(i, nu)], 0, 0))],
        out_specs=pl.BlockSpec((rb, d), lambda i, be, nu: (i, 0)),
    )
    return pl.pallas_call(
        _experts_body,
        grid_spec=grid_spec,
        out_shape=jax.ShapeDtypeStruct((n_rows, d), F32),
        compiler_params=_cparams(("arbitrary",)),
        name="experts",
    )(block_e, n_used, xs, wg, wu, wd)


def _combine_body(tm, pos_ref, posn_ref, ys_hbm, w_ref, h_ref, x1_ref, g2_ref, fg_ref,
                  sg_ref, su_ref, sd_ref, o_ref, buf_ref, sem_ref):
    i = pl.program_id(0)
    nt = pl.num_programs(0)
    slot = lax.rem(i, 2)
    nrow = TOP_K * tm

    def row_copy(src, r, s):
        return pltpu.make_async_copy(ys_hbm.at[pl.ds(src, 1)], buf_ref.at[s, pl.ds(r, 1)], sem_ref.at[s])

    def issue(idx_ref, s):
        def body(r, carry):
            row_copy(idx_ref[0, 0, r], r, s).start()
            return carry
        lax.fori_loop(0, nrow, body, 0)

    @pl.when(i == 0)
    def _():
        issue(pos_ref, 0)

    @pl.when(i + 1 < nt)
    def _():
        issue(posn_ref, 1 - slot)

    pltpu.make_async_copy(ys_hbm.at[pl.ds(0, nrow)], buf_ref.at[slot], sem_ref.at[slot]).wait()

    w = w_ref[...]
    routed = jnp.zeros(h_ref.shape, F32)
    for k in range(TOP_K):
        routed = routed + w[:, k:k + 1] * buf_ref[slot, k * tm:(k + 1) * tm, :]
    xb = h_ref[...].astype(BF16)
    hg = jnp.dot(xb, sg_ref[...], preferred_element_type=F32)
    hu = jnp.dot(xb, su_ref[...], preferred_element_type=F32)
    shared = jnp.dot((_silu(hg) * hu).astype(BF16), sd_ref[...], preferred_element_type=F32)
    x2 = x1_ref[...] + g2_ref[0] * (routed + shared)
    r = lax.rsqrt(jnp.mean(x2 * x2, axis=-1, keepdims=True) + NORM_EPS)
    o_ref[...] = x2 * r * fg_ref[...]


def _tile_slots(pos, tm):
    k, t = pos.shape
    return pos.reshape(k, t // tm, tm).transpose(1, 0, 2).reshape(t // tm, 1, k * tm)


def _combine(ys, pos_t, w6, h2, x1, gate2, final_g, sg, su, sd, tokens_per_batch, tm):
    t, d = h2.shape
    nt = t // tm
    tpb = tokens_per_batch // tm
    ds = sg.shape[1]
    wpad = jnp.zeros((t, LANES), F32).at[:, :TOP_K].set(w6.T)
    tile = pl.BlockSpec((tm, d), lambda i: (i, 0))
    return pl.pallas_call(
        functools.partial(_combine_body, tm),
        grid=(nt,),
        in_specs=[pl.BlockSpec((1, 1, TOP_K * tm), lambda i: (i, 0, 0), memory_space=pltpu.SMEM),
                  pl.BlockSpec((1, 1, TOP_K * tm), lambda i: (jnp.minimum(i + 1, nt - 1), 0, 0),
                               memory_space=pltpu.SMEM),
                  pl.BlockSpec(memory_space=pl.ANY),
                  pl.BlockSpec((tm, LANES), lambda i: (i, 0)),
                  tile, tile,
                  pl.BlockSpec((1, 1, d), lambda i: (i // tpb, 0, 0)),
                  pl.BlockSpec((1, d), lambda i: (0, 0)),
                  pl.BlockSpec((d, ds), lambda i: (0, 0)),
                  pl.BlockSpec((d, ds), lambda i: (0, 0)),
                  pl.BlockSpec((ds, d), lambda i: (0, 0))],
        out_specs=tile,
        out_shape=jax.ShapeDtypeStruct((t, d), F32),
        scratch_shapes=[pltpu.VMEM((2, TOP_K * tm, d), F32), pltpu.SemaphoreType.DMA((2,))],
        compiler_params=_cparams(("arbitrary",)),
        name="moe_combine",
    )(pos_t, pos_t, ys, wpad, h2, x1, gate2, final_g.reshape(1, d), sg, su, sd)


def _ssd_params(ssm_conv_w, ssm_conv_b, ssm_dt_bias, ssm_a_log, ssm_d, ssm_norm_g):
    dx = SSM_HEADS * SSM_HEAD_DIM
    q = SSD_CHUNK
    pad = lambda v: jnp.zeros((2, 1, LANES), F32).at[:, 0, :SSM_HEADS].set(v)
    heads = jnp.arange(SSM_HEADS)
    e1 = (jnp.arange(dx)[None, :] // SSM_HEAD_DIM == heads[:, None]).astype(F32)
    e2 = (jnp.arange(SSM_HEADS * q)[None, :] // q == heads[:, None]).astype(F32)

    def stack(e):
        z = jnp.zeros((2 * LANES, e.shape[1]), F32)
        return z.at[:SSM_HEADS].set(e).at[LANES:LANES + SSM_HEADS].set(e).astype(BF16)

    return dict(
        cwx=ssm_conv_w[:, :dx], cbx=ssm_conv_b[None, :dx],
        cwbc=ssm_conv_w[:, dx:], cbbc=ssm_conv_b[None, dx:],
        dtb=pad(ssm_dt_bias), a=pad(-jnp.exp(ssm_a_log.astype(F32))),
        dexp=jnp.repeat(ssm_d, SSM_HEAD_DIM)[None, :], ng=ssm_norm_g[None, :],
        e=stack(e1), e2=stack(e2))


def kernel(x, c, ctx, c_ctx, ada_w, ada_b, norm1_g, norm2_g, w_in, ssm_conv_w, ssm_conv_b, ssm_dt_bias, ssm_a_log, ssm_d, ssm_norm_g, w_ssm_out, hy_conv_w, hy_conv_b, hy_w1, hy_b1, hy_w2, hy_b2, hy_w3, hy_b3, hy_freq, hy_w_out, hy_decay, hy_bias, w_hy_out, w_o, router_w, router_bias, e_w_gate, e_w_up, e_w_down, sh_w_gate, sh_w_up, sh_w_down, final_g):
    depth = ada_w.shape[0]
    assert depth == 1, "single-layer block"
    b, l, d = x.shape
    lc = ctx.shape[1]
    dx = SSM_HEADS * SSM_HEAD_DIM
    dbc = 2 * SSM_GROUPS * SSM_STATE
    assert dx == d and hy_bias.shape[-1] == d

    rows = -(-(b + 1) // BF16_SUBLANES) * BF16_SUBLANES
    cc = jnp.zeros((rows, d), F32).at[:b].set(c).at[b].set(c_ctx)
    mod_all = _matmul(_silu(cc).astype(BF16), ada_w[0].astype(BF16), F32, rows, 512, "adaln") + ada_b[0]
    mod = mod_all[:b].reshape(b, 6, d)
    mod_c = jnp.broadcast_to(mod_all[b].reshape(1, 6, d), (b, 6, d))

    w = w_in[0]
    o_dt, o_z, o_hy, o_g = dx + dbc, dx + dbc + 2 * SSM_HEADS, 2 * dx + dbc + 2 * SSM_HEADS, 2 * dx + dbc + 2 * SSM_HEADS + 3 * d
    w_main = jnp.concatenate([w[:, o_hy:o_g], w[:, o_g:], w[:, o_z:o_hy], w[:, :dx + dbc]], axis=1).astype(BF16)
    w_dt = (jnp.zeros((d, 2 * LANES), F32)
            .at[:, :SSM_HEADS].set(w[:, o_dt:o_dt + SSM_HEADS])
            .at[:, LANES:LANES + SSM_HEADS].set(w[:, o_dt + SSM_HEADS:o_z])).astype(BF16)
    col_ghy, col_gssd, col_z, col_xs, col_bc = 3, 4, 5, 6, 7 * (d // dbc)

    sp = _ssd_params(ssm_conv_w[0], ssm_conv_b[0], ssm_dt_bias[0], ssm_a_log[0], ssm_d[0], ssm_norm_g[0])

    hc = _normmod(ctx, norm1_g[0], mod_c[:, 0], mod_c[:, 1])
    w_ctx = w_main[:, col_xs * d:]
    proj_c = _matmul(hc.reshape(b * lc, d), w_ctx, BF16, 1024, 512, "ctx_proj").reshape(b, lc, -1)
    dt_c = _matmul(hc.reshape(b * lc, d), w_dt, F32, 1024, 2 * LANES, "ctx_dt").reshape(b, lc, 2 * LANES)
    zero_state = jnp.zeros((b, 2, SSM_STATE, dx), F32)
    ctx_state = _ssd(proj_c, dt_c, zero_state, sp, False, 0, d // dbc, 0)

    hx = _normmod(x, norm1_g[0], mod[:, 0], mod[:, 1]).reshape(b * l, d)
    proj = _matmul(hx, w_main, BF16, 1024, 512, "in_proj").reshape(b, l, -1)
    dt_raw = _matmul(hx, w_dt, F32, 1024, 2 * LANES, "dt_proj").reshape(b, l, 2 * LANES)
    yn = _ssd(proj, dt_raw, ctx_state, sp, True, col_xs, col_bc, col_z)

    fb = min(l, 512)
    ct = 256
    wf, wi = _dft_tables(l, fb)
    hp = dict(hy_w1=hy_w1[0], hy_b1=hy_b1[0], hy_w2=hy_w2[0], hy_b2=hy_b2[0], hy_w3=hy_w3[0], hy_b3=hy_b3[0],
              hy_freq=hy_freq[0], hy_w_out=hy_w_out[0], hy_decay=hy_decay[0])
    coef = _filter_coefs(l, fb, wf, hp)
    zhy = _hyena(proj, hy_conv_w[0], hy_conv_b[0][None, :], wf, wi, coef, hy_bias[0], ct, fb)

    x1, h2 = _outproj(zhy, yn, proj, x, mod[:, 2], mod[:, 3], mod[:, 4], norm2_g[0],
                      w_hy_out[0].astype(BF16), w_ssm_out[0].astype(BF16), w_o[0].astype(BF16),
                      col_ghy, col_gssd)
    x1 = x1.reshape(b * l, d)
    h2 = h2.reshape(b * l, d)

    e6, c6, w6, counts = _route(h2, router_w[0], router_bias[0])
    pos, block_e, n_used, zero_start, zero_n, nb = _dispatch_plan(e6, c6, counts, b * l * TOP_K)
    tm = min(l, 128)
    pos_t = _tile_slots(pos, tm)
    xs = _dispatch(h2, pos_t, zero_start, zero_n, n_used, nb, tm)
    ys = _experts(xs, block_e, n_used, e_w_gate[0].astype(BF16), e_w_up[0].astype(BF16),
                  e_w_down[0].astype(BF16))
    out = _combine(ys, pos_t, w6, h2, x1, mod[:, 5].reshape(b, 1, d), final_g,
                   sh_w_gate[0].astype(BF16), sh_w_up[0].astype(BF16), sh_w_down[0].astype(BF16), l, tm)
    return out.reshape(b, l, d)
```

```python
import functools
import math

import jax
import jax.numpy as jnp
from jax import lax
from jax.experimental import pallas as pl
from jax.experimental.pallas import tpu as pltpu

F32 = jnp.float32
BF16 = jnp.bfloat16

NORM_EPS = 1e-6
SSM_HEADS = 16
SSM_HEAD_DIM = 64
SSM_GROUPS = 2
SSM_STATE = 128
SSD_CHUNK = 128
HY_ORDER = 2
HY_BANDS = 16
N_EXPERTS = 64
TOP_K = 6
N_EXPERT_GROUPS = 8
TOPK_GROUPS = 4
ROUTED_SCALE = 2.5
EXPERT_BLOCK = 256

LANES = 128
BF16_SUBLANES = 16
VMEM_LIMIT = 56 * 1024 * 1024


def _cparams(sem):
    return pltpu.CompilerParams(dimension_semantics=sem, vmem_limit_bytes=VMEM_LIMIT)


def _silu(x):
    return x * jax.nn.sigmoid(x)


def _split_bf16(q):
    hi = q.astype(BF16)
    lo = (q - hi.astype(F32)).astype(BF16)
    return hi, lo


def _normmod_body(x_ref, g_ref, sh_ref, sc_ref, o_ref):
    x = x_ref[0]
    r = lax.rsqrt(jnp.mean(x * x, axis=-1, keepdims=True) + NORM_EPS)
    y = x * r * g_ref[...]
    o_ref[0] = (y * (1.0 + sc_ref[0]) + sh_ref[0]).astype(o_ref.dtype)


def _normmod(x, g, shift, scale, out_dtype=BF16):
    b, l, d = x.shape
    tl = min(l, 512)
    return pl.pallas_call(
        _normmod_body,
        grid=(b, l // tl),
        in_specs=[pl.BlockSpec((1, tl, d), lambda i, j: (i, j, 0)),
                  pl.BlockSpec((1, d), lambda i, j: (0, 0)),
                  pl.BlockSpec((1, 1, d), lambda i, j: (i, 0, 0)),
                  pl.BlockSpec((1, 1, d), lambda i, j: (i, 0, 0))],
        out_specs=pl.BlockSpec((1, tl, d), lambda i, j: (i, j, 0)),
        out_shape=jax.ShapeDtypeStruct((b, l, d), out_dtype),
        compiler_params=_cparams(("parallel", "parallel")),
        name="normmod",
    )(x, g.reshape(1, d), shift.reshape(b, 1, d), scale.reshape(b, 1, d))


def _mm_body(a_ref, w_ref, o_ref):
    o_ref[...] = jnp.dot(a_ref[...], w_ref[...], preferred_element_type=F32).astype(o_ref.dtype)


def _matmul(a, w, out_dtype, tm, tn, name):
    m, k = a.shape
    n = w.shape[1]
    tm, tn = min(tm, m), min(tn, n)
    assert m % tm == 0 and n % tn == 0, (m, n, tm, tn)
    return pl.pallas_call(
        _mm_body,
        grid=(m // tm, n // tn),
        in_specs=[pl.BlockSpec((tm, k), lambda i, j: (i, 0)),
                  pl.BlockSpec((k, tn), lambda i, j: (0, j))],
        out_specs=pl.BlockSpec((tm, tn), lambda i, j: (i, j)),
        out_shape=jax.ShapeDtypeStruct((m, n), out_dtype),
        compiler_params=_cparams(("parallel", "arbitrary")),
        name=name,
    )(a, w)


def _conv3(u, prev_row, next_row, w_ref, b_ref):
    q = u.shape[0]
    row = lax.broadcasted_iota(jnp.int32, u.shape, 0)
    um1 = jnp.where(row == 0, prev_row, pltpu.roll(u, 1, 0))
    up1 = jnp.where(row == q - 1, next_row, pltpu.roll(u, q - 1, 0))
    return um1 * w_ref[0:1, :] + u * w_ref[1:2, :] + up1 * w_ref[2:3, :] + b_ref[...]


def _ssd_body(nc, emit_y, xs_ref, xsp_ref, xsn_ref, bc_ref, bcp_ref, bcn_ref, dt_ref, z_ref,
              cwx_ref, cbx_ref, cwbc_ref, cbbc_ref, dtb_ref, a_ref, dexp_ref, ng_ref,
              e_ref, e2_ref, init_ref, o_ref, st_ref, ysc_ref):
    q = SSD_CHUNK
    p = pl.program_id(1)
    c = pl.program_id(2)
    is_fwd = p == 1
    ci = jnp.where(is_fwd, c, nc - 1 - c)

    @pl.when(c == 0)
    def _():
        st_ref[...] = init_ref[0, 0]

    keep_prev = (ci > 0).astype(F32)
    keep_next = (ci < nc - 1).astype(F32)
    hrow = BF16_SUBLANES - 1
    xs = _silu(_conv3(xs_ref[0].astype(F32),
                      xsp_ref[0].astype(F32)[hrow:hrow + 1] * keep_prev,
                      xsn_ref[0].astype(F32)[0:1] * keep_next, cwx_ref, cbx_ref))
    bc = _silu(_conv3(bc_ref[0].astype(F32),
                      bcp_ref[0].astype(F32)[hrow:hrow + 1] * keep_prev,
                      bcn_ref[0].astype(F32)[0:1] * keep_next, cwbc_ref, cbbc_ref))
    ng2 = SSM_GROUPS * SSM_STATE
    bm, cm = bc[:, :ng2], bc[:, ng2:]

    dtr = dt_ref[0] + dtb_ref[0]
    dt = jnp.maximum(dtr, 0.0) + jnp.log(1.0 + jnp.exp(-jnp.abs(dtr)))
    a = dt * a_ref[0]

    rr = lax.broadcasted_iota(jnp.int32, (q, q), 0)
    cc = lax.broadcasted_iota(jnp.int32, (q, q), 1)
    mask = (rr - cc) * jnp.where(is_fwd, 1, -1) >= 0
    tri = mask.astype(BF16)
    a_hi, a_lo = _split_bf16(a)
    acs2 = jnp.dot(tri, jnp.concatenate([a_hi, a_lo], axis=1), preferred_element_type=F32)
    acs = acs2[:, :LANES] + acs2[:, LANES:]
    c_hi, c_lo = _split_bf16(acs)
    acs_cat = jnp.concatenate([c_hi, c_lo], axis=1)
    ex = jnp.dot(acs_cat, e_ref[...], preferred_element_type=F32)
    d_hi, d_lo = _split_bf16(dt)
    dt_exp = jnp.dot(jnp.concatenate([d_hi, d_lo], axis=1), e_ref[...], preferred_element_type=F32)
    ex_end = jnp.where(is_fwd, ex[q - 1:q, :], ex[0:1, :])

    xdt = xs * dt_exp
    xw = (xdt * jnp.exp(ex_end - ex)).astype(BF16)
    state = st_ref[...]

    if emit_y:
        acs_t = acs.T
        cb_all = jnp.dot(acs_cat, e2_ref[...], preferred_element_type=F32)
        lane = lax.broadcasted_iota(jnp.int32, (q, LANES), 1)
        state_bf = state.astype(BF16)
        hpg = SSM_HEADS // SSM_GROUPS
        gw = hpg * SSM_HEAD_DIM
        pieces = []
        for g in range(SSM_GROUPS):
            cg = cm[:, g * SSM_STATE:(g + 1) * SSM_STATE].astype(BF16)
            bg = bm[:, g * SSM_STATE:(g + 1) * SSM_STATE].astype(BF16)
            cbg = lax.dot_general(cg, bg, (((1,), (1,)), ((), ())), preferred_element_type=F32)
            yoff = jnp.dot(cg, state_bf[:, g * gw:(g + 1) * gw], preferred_element_type=F32)
            yoff = yoff * jnp.exp(ex[:, g * gw:(g + 1) * gw])
            for j in range(hpg // 2):
                h0 = g * hpg + 2 * j
                ms = []
                for h in (h0, h0 + 1):
                    seg = cb_all[:, h * q:(h + 1) * q] - acs_t[h:h + 1, :]
                    ms.append((cbg * jnp.exp(jnp.where(mask, seg, -jnp.inf))).astype(BF16))
                m2 = jnp.concatenate(ms, axis=1)
                xp = xdt[:, h0 * SSM_HEAD_DIM:(h0 + 2) * SSM_HEAD_DIM]
                x2 = jnp.concatenate([jnp.where(lane < SSM_HEAD_DIM, xp, 0.0),
                                      jnp.where(lane >= SSM_HEAD_DIM, xp, 0.0)], axis=0).astype(BF16)
                yd = jnp.dot(m2, x2, preferred_element_type=F32)
                pieces.append(yd + yoff[:, 2 * j * SSM_HEAD_DIM:(2 * j + 2) * SSM_HEAD_DIM])
        y = jnp.concatenate(pieces, axis=1)
        row0 = pl.multiple_of(ci * q, q)

        @pl.when(p == 0)
        def _():
            ysc_ref[pl.ds(row0, q), :] = y

        @pl.when(p == 1)
        def _():
            yt = y + ysc_ref[pl.ds(row0, q), :] + xs * dexp_ref[...]
            v = yt * _silu(z_ref[0].astype(F32))
            outs = []
            for g in range(SSM_GROUPS):
                vg = v[:, g * gw:(g + 1) * gw]
                outs.append(vg * lax.rsqrt(jnp.mean(vg * vg, axis=-1, keepdims=True) + NORM_EPS))
            o_ref[0] = (jnp.concatenate(outs, axis=1) * ng_ref[...]).astype(o_ref.dtype)

    new_parts = []
    hpg = SSM_HEADS // SSM_GROUPS
    gw = hpg * SSM_HEAD_DIM
    for g in range(SSM_GROUPS):
        bt = bm[:, g * SSM_STATE:(g + 1) * SSM_STATE].T.astype(BF16)
        new_parts.append(jnp.dot(bt, xw[:, g * gw:(g + 1) * gw], preferred_element_type=F32))
    st_new = state * jnp.exp(ex_end) + jnp.concatenate(new_parts, axis=1)
    st_ref[...] = st_new
    if not emit_y:
        @pl.when(c == nc - 1)
        def _():
            o_ref[0, 0] = st_new


def _ssd(proj, dt_raw, init, prm, emit_y, xs_col, bc_col, z_col):
    b, l, _ = proj.shape
    q = SSD_CHUNK
    nc = l // q
    hb = q // BF16_SUBLANES
    nh = l // BF16_SUBLANES
    dx = SSM_HEADS * SSM_HEAD_DIM
    dbc = 2 * SSM_GROUPS * SSM_STATE

    def ci_of(p, c):
        return p * c + (1 - p) * (nc - 1 - c)

    def main(col):
        return lambda i, p, c: (i, ci_of(p, c), col)

    def prev(col):
        return lambda i, p, c: (i, jnp.maximum(ci_of(p, c) * hb - 1, 0), col)

    def nxt(col):
        return lambda i, p, c: (i, jnp.minimum(ci_of(p, c) * hb + hb, nh - 1), col)

    const2 = lambda i, p, c: (0, 0)
    bydir = lambda i, p, c: (1 - p, 0, 0)
    in_specs = [
        pl.BlockSpec((1, q, dx), main(xs_col)),
        pl.BlockSpec((1, BF16_SUBLANES, dx), prev(xs_col)),
        pl.BlockSpec((1, BF16_SUBLANES, dx), nxt(xs_col)),
        pl.BlockSpec((1, q, dbc), main(bc_col)),
        pl.BlockSpec((1, BF16_SUBLANES, dbc), prev(bc_col)),
        pl.BlockSpec((1, BF16_SUBLANES, dbc), nxt(bc_col)),
        pl.BlockSpec((1, q, LANES), lambda i, p, c: (i, ci_of(p, c), 1 - p)),
        pl.BlockSpec((1, q, dx), (lambda i, p, c: (i, c * p, z_col))),
        pl.BlockSpec((3, dx), const2), pl.BlockSpec((1, dx), const2),
        pl.BlockSpec((3, dbc), const2), pl.BlockSpec((1, dbc), const2),
        pl.BlockSpec((1, 1, LANES), bydir), pl.BlockSpec((1, 1, LANES), bydir),
        pl.BlockSpec((1, dx), const2), pl.BlockSpec((1, dx), const2),
        pl.BlockSpec((2 * LANES, dx), const2),
        pl.BlockSpec((2 * LANES, SSM_HEADS * q), const2),
        pl.BlockSpec((1, 1, SSM_STATE, dx), lambda i, p, c: (i, 1 - p, 0, 0)),
    ]
    if emit_y:
        out_spec = pl.BlockSpec((1, q, dx), lambda i, p, c: (i, c * p, 0))
        out_shape = jax.ShapeDtypeStruct((b, l, dx), BF16)
    else:
        out_spec = pl.BlockSpec((1, 1, SSM_STATE, dx), lambda i, p, c: (i, 1 - p, 0, 0))
        out_shape = jax.ShapeDtypeStruct((b, 2, SSM_STATE, dx), F32)
    return pl.pallas_call(
        functools.partial(_ssd_body, nc, emit_y),
        grid=(b, 2, nc),
        in_specs=in_specs,
        out_specs=out_spec,
        out_shape=out_shape,
        scratch_shapes=[pltpu.VMEM((SSM_STATE, dx), F32),
                        pltpu.VMEM((l if emit_y else q, dx), F32)],
        compiler_params=_cparams(("parallel", "arbitrary", "arbitrary")),
        name="ssd_scan" if emit_y else "ssd_ctx_state",
    )(proj, proj, proj, proj, proj, proj, dt_raw, proj,
      prm["cwx"], prm["cbx"], prm["cwbc"], prm["cbbc"], prm["dtb"], prm["a"], prm["dexp"], prm["ng"],
      prm["e"], prm["e2"], init)


def _hyena_body(nfb, fb, x1_ref, x2_ref, v_ref, cw1_ref, cw2_ref, cw3_ref, cb1_ref, cb2_ref, cb3_ref,
                wf_ref, wi_ref, coef_ref, skip_ref, o_ref, zc_ref, zf_ref, acc_ref, x1c_ref, x2c_ref):
    o = pl.program_id(2)
    k = pl.program_id(3)

    @pl.when((o == 0) & (k == 0))
    def _():
        l, ct = x1c_ref.shape
        rows = min(l, 256)
        halo = BF16_SUBLANES
        zero = jnp.zeros((1, ct), F32)
        for r0 in range(0, l, rows):
            lo, hi = max(r0 - halo, 0), min(r0 + rows + halo, l)
            mid = slice(r0 - lo, r0 - lo + rows)
            x1c_ref[r0:r0 + rows, :] = _conv3(x1_ref[0, lo:hi, :].astype(F32), zero, zero, cw1_ref, cb1_ref)[mid]
            x2c_ref[r0:r0 + rows, :] = _conv3(x2_ref[0, lo:hi, :].astype(F32), zero, zero, cw2_ref, cb2_ref)[mid]
            vc = _conv3(v_ref[0, lo:hi, :].astype(F32), zero, zero, cw3_ref, cb3_ref)[mid]
            zf_ref[r0:r0 + rows, :] = vc
            zc_ref[r0:r0 + rows, :] = vc.astype(BF16)

    @pl.when(k == 0)
    def _():
        acc_ref[...] = jnp.zeros_like(acc_ref)

    u = jnp.dot(wf_ref[...], zc_ref[...], preferred_element_type=F32)
    ure, uim = u[:fb], u[fb:]
    yre = ure * coef_ref[0, 0, 0, 0] - uim * coef_ref[0, 0, 0, 1]
    yim = ure * coef_ref[0, 0, 0, 2] + uim * coef_ref[0, 0, 0, 3]
    y = jnp.concatenate([yre, yim], axis=0).astype(BF16)
    acc_ref[...] += jnp.dot(wi_ref[0], y, preferred_element_type=F32)

    @pl.when(k == nfb - 1)
    def _():
        skip = jnp.where(o == 0, skip_ref[0:1, :], skip_ref[1:2, :])
        lc = acc_ref[...] + zf_ref[...] * skip

        @pl.when(o == 0)
        def _():
            zn = x1c_ref[...] * lc
            zf_ref[...] = zn
            zc_ref[...] = zn.astype(BF16)

        @pl.when(o == 1)
        def _():
            o_ref[0] = (x2c_ref[...] * lc).astype(o_ref.dtype)


def _hyena(proj, cw, cb, wf, wi, coef, skip, ct, fb):
    b, l, _ = proj.shape
    d = skip.shape[1]
    nct = d // ct
    nfb = l // fb

    def col(part):
        return lambda j, i, o, k: (i, 0, part * nct + j)

    def wcol(part):
        return lambda j, i, o, k: (0, part * nct + j)

    return pl.pallas_call(
        functools.partial(_hyena_body, nfb, fb),
        grid=(nct, b, HY_ORDER, nfb),
        in_specs=[pl.BlockSpec((1, l, ct), col(0)), pl.BlockSpec((1, l, ct), col(1)),
                  pl.BlockSpec((1, l, ct), col(2)),
                  pl.BlockSpec((3, ct), wcol(0)), pl.BlockSpec((3, ct), wcol(1)), pl.BlockSpec((3, ct), wcol(2)),
                  pl.BlockSpec((1, ct), wcol(0)), pl.BlockSpec((1, ct), wcol(1)), pl.BlockSpec((1, ct), wcol(2)),
                  pl.BlockSpec((2 * fb, l), lambda j, i, o, k: (k, 0)),
                  pl.BlockSpec((1, l, 2 * fb), lambda j, i, o, k: (k, 0, 0)),
                  pl.BlockSpec((1, 1, 1, 4, fb, ct), lambda j, i, o, k: (o, j, k, 0, 0, 0)),
                  pl.BlockSpec((HY_ORDER, ct), lambda j, i, o, k: (0, j))],
        out_specs=pl.BlockSpec((1, l, ct), lambda j, i, o, k: (i, 0, j)),
        out_shape=jax.ShapeDtypeStruct((b, l, d), BF16),
        scratch_shapes=[pltpu.VMEM((l, ct), BF16), pltpu.VMEM((l, ct), F32), pltpu.VMEM((l, ct), F32),
                        pltpu.VMEM((l, ct), F32), pltpu.VMEM((l, ct), F32)],
        compiler_params=_cparams(("parallel", "parallel", "arbitrary", "arbitrary")),
        name="hyena",
    )(proj, proj, proj, cw, cw, cw, cb, cb, cb, wf, wi, coef, skip)


def _dft_tables(l, fb):
    n = 2 * l
    f = jnp.arange(l, dtype=jnp.int32)
    t = jnp.arange(l, dtype=jnp.int32)
    alt = jnp.where(t % 2 == 0, 1.0, -1.0).astype(F32)

    def cis(mult, count):
        j = jnp.arange(count, dtype=jnp.int32)
        ang = (2.0 * math.pi / n) * ((j[:, None] * mult * t[None, :]) % n).astype(F32)
        return jnp.cos(ang), jnp.sin(ang)

    def cmul(a, b):
        (ca, sa), (cb_, sb) = a, b
        c = ca[:, None] * cb_[None] - sa[:, None] * sb[None]
        s = sa[:, None] * cb_[None] + ca[:, None] * sb[None]
        return c.reshape(-1, l), s.reshape(-1, l)

    digit = 16
    cos, sin = cmul(cis(digit * digit, -(-l // (digit * digit))), cmul(cis(digit, digit), cis(1, digit)))
    cos, sin = cos[:l], sin[:l]
    f_re = cos
    f_im = jnp.where(f[:, None] == 0, alt[None, :], -sin)
    cf = jnp.where(f[:, None] == 0, 1.0 / n, 2.0 / n).astype(F32)
    i_re = cf * cos
    i_im = jnp.where(f[:, None] == 0, alt[None, :] / n, -(2.0 / n) * sin)

    def blocked(re, im):
        return jnp.concatenate([re.reshape(l // fb, 1, fb, l), im.reshape(l // fb, 1, fb, l)], axis=1).reshape(n, l)

    wf = blocked(f_re, f_im).astype(BF16)
    wi = blocked(i_re, i_im).reshape(l // fb, 2 * fb, l).transpose(0, 2, 1).astype(BF16)
    return wf, wi


def _implicit_filters(l, hp):
    hi = lax.Precision.HIGHEST
    pos = jnp.arange(l, dtype=F32)
    t = pos / max(l - 1, 1)
    bands = jnp.linspace(1e-4, HY_BANDS - 1, HY_BANDS, dtype=F32)
    ang = (2.0 * math.pi / l) * pos[:, None] * bands[None, :]
    z = jnp.concatenate([t[:, None], jnp.cos(ang), -jnp.sin(ang)], axis=-1)
    freq = hp["hy_freq"]
    hdn = jnp.sin(freq[0] * (jnp.dot(z, hp["hy_w1"], precision=hi) + hp["hy_b1"]))
    hdn = jnp.sin(freq[1] * (jnp.dot(hdn, hp["hy_w2"], precision=hi) + hp["hy_b2"]))
    hdn = jnp.sin(freq[2] * (jnp.dot(hdn, hp["hy_w3"], precision=hi) + hp["hy_b3"]))
    d = hp["hy_decay"].shape[-1]
    h = jnp.dot(hdn, hp["hy_w_out"], precision=hi).reshape(l, 2, HY_ORDER, d)
    window = jnp.exp(-t[:, None, None] * jnp.abs(hp["hy_decay"]))
    return h * window[:, None]


def _filter_coefs(l, fb, ct, wf, hp):
    h = _implicit_filters(l, hp)
    d = h.shape[-1]
    hf = h[:, 0]
    hb = h[:, 1].at[0].set(0.0)
    hcat = jnp.concatenate([hf.reshape(l, HY_ORDER * d), hb.reshape(l, HY_ORDER * d)], axis=1).astype(BF16)
    s = _matmul(wf, hcat, F32, 512, 512, "filter_dft")
    s = s.reshape(l // fb, 2, fb, 2, HY_ORDER, d)
    s = jnp.moveaxis(s, 1, 0).reshape(2, l, 2, HY_ORDER, d)
    kre = s[0, :, 0] + s[0, :, 1]
    kim = s[1, :, 0] - s[1, :, 1]
    knyq = s[1, 0, 0] + s[1, 0, 1]
    c1 = kim.at[0].set(0.0)
    c3 = kre.at[0].set(knyq)
    coef = jnp.stack([kre, c1, c1, c3], axis=0)
    coef = coef.reshape(4, l // fb, fb, HY_ORDER, d // ct, ct)
    return coef.transpose(3, 4, 1, 0, 2, 5)


def _outproj_body(zhy_ref, yn_ref, ghy_ref, gssd_ref, x_ref, g1_ref, sh2_ref, sc2_ref, n2g_ref,
                  why_ref, wssm_ref, wo_ref, x1_ref, h2_ref):
    hyo = jnp.dot(zhy_ref[0], why_ref[...], preferred_element_type=F32)
    sso = jnp.dot(yn_ref[0], wssm_ref[...], preferred_element_type=F32)
    merged = (jax.nn.sigmoid(ghy_ref[0].astype(F32)) * hyo
              + jax.nn.sigmoid(gssd_ref[0].astype(F32)) * sso)
    xmix = jnp.dot(merged.astype(BF16), wo_ref[...], preferred_element_type=F32)
    x1 = x_ref[0] + g1_ref[0] * xmix
    x1_ref[0] = x1
    r = lax.rsqrt(jnp.mean(x1 * x1, axis=-1, keepdims=True) + NORM_EPS)
    h2_ref[0] = (x1 * r * n2g_ref[...]) * (1.0 + sc2_ref[0]) + sh2_ref[0]


def _outproj(zhy, yn, proj, x, gate1, shift2, scale2, n2g, why, wssm, wo, ghy_col, gssd_col):
    b, l, d = x.shape
    tm = min(l, 512)
    tile = lambda col: pl.BlockSpec((1, tm, d), lambda i, j: (i, j, col))
    row = pl.BlockSpec((1, 1, d), lambda i, j: (i, 0, 0))
    wspec = pl.BlockSpec((d, d), lambda i, j: (0, 0))
    return pl.pallas_call(
        _outproj_body,
        grid=(b, l // tm),
        in_specs=[tile(0), tile(0), tile(ghy_col), tile(gssd_col), tile(0), row, row, row,
                  pl.BlockSpec((1, d), lambda i, j: (0, 0)), wspec, wspec, wspec],
        out_specs=[tile(0), tile(0)],
        out_shape=[jax.ShapeDtypeStruct((b, l, d), F32), jax.ShapeDtypeStruct((b, l, d), F32)],
        compiler_params=_cparams(("parallel", "parallel")),
        name="merge_outproj",
    )(zhy, yn, proj, proj, x, gate1.reshape(b, 1, d), shift2.reshape(b, 1, d), scale2.reshape(b, 1, d),
      n2g.reshape(1, d), why, wssm, wo)


PER_GROUP = N_EXPERTS // N_EXPERT_GROUPS


def _route_body(tm, h_ref, wh_ref, wl_ref, bias_ref, e_ref, c_ref, w_ref, cnt_ref, carry_ref):
    ng, pg = N_EXPERT_GROUPS, PER_GROUP

    @pl.when(pl.program_id(0) == 0)
    def _():
        carry_ref[...] = jnp.zeros_like(carry_ref)

    x_hi, x_lo = _split_bf16(h_ref[...])
    nt = (((1,), (1,)), ((), ()))
    lg = (lax.dot_general(wh_ref[...], x_hi, nt, preferred_element_type=F32)
          + lax.dot_general(wh_ref[...], x_lo, nt, preferred_element_type=F32)
          + lax.dot_general(wl_ref[...], x_hi, nt, preferred_element_type=F32))
    sc = jax.nn.sigmoid(lg)
    sel = sc + bias_ref[...]

    mem = [sel[m * ng:(m + 1) * ng, :] for m in range(pg)]
    m1 = functools.reduce(jnp.maximum, mem)
    first = functools.reduce(jnp.minimum, [jnp.where(mem[m] == m1, m, pg) for m in range(pg)])
    m2 = functools.reduce(jnp.maximum, [jnp.where(first == m, -jnp.inf, mem[m]) for m in range(pg)])
    gs = m1 + m2
    gid = lax.broadcasted_iota(jnp.int32, gs.shape, 0)
    beats = jnp.zeros(gs.shape, jnp.int32)
    for g2 in range(ng):
        v = gs[g2:g2 + 1, :]
        beats = beats + ((v > gs) | ((v == gs) & (g2 < gid))).astype(jnp.int32)
    keep = beats < TOPK_GROUPS
    selm = jnp.concatenate([jnp.where(keep, mem[m], -jnp.inf) for m in range(pg)], axis=0)

    r = lax.broadcasted_iota(jnp.int32, selm.shape, 0)
    eid = (r % ng) * pg + r // ng
    rank = jnp.zeros(selm.shape, jnp.int32)
    for r2 in range(N_EXPERTS):
        v = selm[r2:r2 + 1, :]
        e2 = (r2 % ng) * pg + r2 // ng
        rank = rank + ((v > selm) | ((v == selm) & (e2 < eid))).astype(jnp.int32)
    chosen = rank < TOP_K
    wsel = jnp.where(chosen, sc, 0.0)
    wn = wsel / jnp.sum(wsel, axis=0, keepdims=True) * ROUTED_SCALE

    cb = chosen.astype(BF16)
    t0 = lax.broadcasted_iota(jnp.int32, (tm, tm), 0)
    t1 = lax.broadcasted_iota(jnp.int32, (tm, tm), 1)
    before = jnp.dot(cb, (t0 < t1).astype(BF16), preferred_element_type=F32) + carry_ref[...]
    carry = carry_ref[...] + jnp.dot(cb, jnp.ones((tm, tm), BF16), preferred_element_type=F32)
    carry_ref[...] = carry
    cnt_ref[...] = carry

    k8 = lax.broadcasted_iota(jnp.int32, e_ref.shape, 0)
    eidf = eid.astype(F32)
    e_out = jnp.zeros(e_ref.shape, F32)
    c_out = jnp.zeros(e_ref.shape, F32)
    w_out = jnp.zeros(e_ref.shape, F32)
    for k in range(TOP_K):
        mk = rank == k
        e_out = jnp.where(k8 == k, jnp.sum(jnp.where(mk, eidf, 0.0), axis=0, keepdims=True), e_out)
        c_out = jnp.where(k8 == k, jnp.sum(jnp.where(mk, before, 0.0), axis=0, keepdims=True), c_out)
        w_out = jnp.where(k8 == k, jnp.sum(jnp.where(mk, wn, 0.0), axis=0, keepdims=True), w_out)
    e_ref[...] = e_out.astype(jnp.int32)
    c_ref[...] = c_out.astype(jnp.int32)
    w_ref[...] = w_out


def _route(h2, router_w, router_bias):
    t, d = h2.shape
    tm = min(t, 256)
    ng, pg = N_EXPERT_GROUPS, PER_GROUP
    wt = router_w.T.reshape(ng, pg, d).transpose(1, 0, 2).reshape(N_EXPERTS, d)
    w_hi, w_lo = _split_bf16(wt)
    bias = jnp.broadcast_to(router_bias.astype(F32).reshape(ng, pg).T.reshape(N_EXPERTS, 1), (N_EXPERTS, tm))
    slot = pl.BlockSpec((8, tm), lambda i: (0, i))
    const = lambda shape: pl.BlockSpec(shape, lambda i: (0, 0))
    e8, c8, w8, cnt = pl.pallas_call(
        functools.partial(_route_body, tm),
        grid=(t // tm,),
        in_specs=[pl.BlockSpec((tm, d), lambda i: (i, 0)), const((N_EXPERTS, d)), const((N_EXPERTS, d)),
                  const((N_EXPERTS, tm))],
        out_specs=[slot, slot, slot, const((N_EXPERTS, tm))],
        out_shape=[jax.ShapeDtypeStruct((8, t), jnp.int32), jax.ShapeDtypeStruct((8, t), jnp.int32),
                   jax.ShapeDtypeStruct((8, t), F32), jax.ShapeDtypeStruct((N_EXPERTS, tm), F32)],
        scratch_shapes=[pltpu.VMEM((N_EXPERTS, tm), F32)],
        compiler_params=_cparams(("arbitrary",)),
        name="moe_route",
    )(h2, w_hi, w_lo, bias)
    counts = cnt[:, 0].reshape(pg, ng).T.reshape(N_EXPERTS).astype(jnp.int32)
    return e8[:TOP_K], c8[:TOP_K], w8[:TOP_K], counts


def _dispatch_plan(e6, c6, counts, n_slots):
    padded = (counts + EXPERT_BLOCK - 1) // EXPERT_BLOCK * EXPERT_BLOCK
    pad_end = jnp.cumsum(padded)
    pad_start = pad_end - padded
    nb = -(-n_slots // EXPERT_BLOCK) + N_EXPERTS
    blk = jnp.arange(nb, dtype=jnp.int32) * EXPERT_BLOCK
    block_e = jnp.minimum(jnp.sum((pad_end[None, :] <= blk[:, None]).astype(jnp.int32), axis=1), N_EXPERTS - 1)
    n_used = (pad_end[-1] // EXPERT_BLOCK).astype(jnp.int32).reshape(1)
    onehot = e6[..., None] == jnp.arange(N_EXPERTS, dtype=jnp.int32)
    pos = jnp.sum(jnp.where(onehot, pad_start, 0), axis=-1) + c6
    return pos.astype(jnp.int32), block_e.astype(jnp.int32), n_used, (pad_start + counts).astype(jnp.int32), \
        (padded - counts).astype(jnp.int32), nb


def _dispatch_body(tm, nt, nb, zs_ref, zn_ref, nu_ref, pos_ref, h_hbm, xs_hbm, buf_ref, zero_ref,
                   lsem_ref, sem_ref, zsem_ref):
    i = pl.program_id(0)
    slot = lax.rem(i, 3)
    nrow = TOP_K * tm

    def load(j, s):
        return pltpu.make_async_copy(h_hbm.at[pl.ds(pl.multiple_of(j * tm, tm), tm)], buf_ref.at[s], lsem_ref.at[s])

    @pl.when(i == 0)
    def _():
        load(0, 0).start()
        if nt > 1:
            load(1, 1).start()

    load(i, slot).wait()

    def body(r, carry):
        for k in range(TOP_K):
            pltpu.make_async_copy(buf_ref.at[slot, pl.ds(r, 1)],
                                  xs_hbm.at[pl.ds(pos_ref[0, 0, k * tm + r], 1)], sem_ref.at[slot]).start()
        return carry
    lax.fori_loop(0, tm, body, 0, unroll=8)

    def wait_batch(s):
        pltpu.make_async_copy(h_hbm.at[pl.ds(0, nrow)], xs_hbm.at[pl.ds(0, nrow)], sem_ref.at[s]).wait()

    nxt = lax.rem(i + 2, 3)

    @pl.when(i > 0)
    def _():
        wait_batch(nxt)

    @pl.when(i + 2 < nt)
    def _():
        load(i + 2, nxt).start()

    @pl.when(i == nt - 1)
    def _():
        wait_batch(slot)
        zero_ref[...] = jnp.zeros_like(zero_ref)

        def pad_copy(row):
            return pltpu.make_async_copy(zero_ref.at[pl.ds(0, 1)], xs_hbm.at[pl.ds(row, 1)], zsem_ref.at[0])

        def tail_copy(blk):
            row = pl.multiple_of(blk * EXPERT_BLOCK, EXPERT_BLOCK)
            return pltpu.make_async_copy(zero_ref, xs_hbm.at[pl.ds(row, EXPERT_BLOCK)], zsem_ref.at[1])

        def tail_start(blk, carry):
            tail_copy(blk).start()
            return carry
        lax.fori_loop(nu_ref[0], nb, tail_start, 0)

        def tail_wait(blk, carry):
            tail_copy(blk).wait()
            return carry
        lax.fori_loop(nu_ref[0], nb, tail_wait, 0)

        def per_expert(e, carry):
            def start(j, c2):
                pad_copy(zs_ref[e] + j).start()
                return c2
            lax.fori_loop(0, zn_ref[e], start, 0)

            def wait(j, c2):
                pad_copy(zs_ref[e] + j).wait()
                return c2
            lax.fori_loop(0, zn_ref[e], wait, 0)
            return carry
        lax.fori_loop(0, N_EXPERTS, per_expert, 0)


def _dispatch(h2, pos_t, zero_start, zero_n, n_used, nb, tm):
    t, d = h2.shape
    grid_spec = pltpu.PrefetchScalarGridSpec(
        num_scalar_prefetch=3,
        grid=(t // tm,),
        in_specs=[pl.BlockSpec((1, 1, TOP_K * tm), lambda i, zs, zn, nu: (i, 0, 0), memory_space=pltpu.SMEM),
                  pl.BlockSpec(memory_space=pl.ANY)],
        out_specs=pl.BlockSpec(memory_space=pl.ANY),
        scratch_shapes=[pltpu.VMEM((3, tm, d), F32), pltpu.VMEM((EXPERT_BLOCK, d), F32),
                        pltpu.SemaphoreType.DMA((3,)), pltpu.SemaphoreType.DMA((3,)),
                        pltpu.SemaphoreType.DMA((2,))],
    )
    return pl.pallas_call(
        functools.partial(_dispatch_body, tm, t // tm, nb),
        grid_spec=grid_spec,
        out_shape=jax.ShapeDtypeStruct((nb * EXPERT_BLOCK, d), F32),
        compiler_params=_cparams(("arbitrary",)),
        name="moe_dispatch",
    )(zero_start, zero_n, n_used, pos_t, h2)


def _experts_body(be_ref, nu_ref, x_ref, wg_ref, wu_ref, wd_ref, o_ref):
    used = pl.program_id(0) < nu_ref[0]

    @pl.when(used)
    def _():
        x = x_ref[...].astype(BF16)
        hg = jnp.dot(x, wg_ref[0], preferred_element_type=F32)
        hu = jnp.dot(x, wu_ref[0], preferred_element_type=F32)
        act = (_silu(hg) * hu).astype(BF16)
        o_ref[...] = jnp.dot(act, wd_ref[0], preferred_element_type=F32)

    @pl.when(jnp.logical_not(used))
    def _():
        o_ref[...] = jnp.zeros_like(o_ref)


def _experts(xs, block_e, n_used, wg, wu, wd):
    n_rows, d = xs.shape
    de = wg.shape[2]
    rb = EXPERT_BLOCK
    last = lambda i, nu: jnp.minimum(i, nu[0] - 1)
    grid_spec = pltpu.PrefetchScalarGridSpec(
        num_scalar_prefetch=2,
        grid=(n_rows // rb,),
        in_specs=[pl.BlockSpec((rb, d), lambda i, be, nu: (last(i, nu), 0)),
                  pl.BlockSpec((1, d, de), lambda i, be, nu: (be[last(i, nu)], 0, 0)),
                  pl.BlockSpec((1, d, de), lambda i, be, nu: (be[last(i, nu)], 0, 0)),
                  pl.BlockSpec((1, de, d), lambda i, be, nu: (be[last(i, nu)], 0, 0))],
        out_specs=pl.BlockSpec((rb, d), lambda i, be, nu: (i, 0)),
    )
    return pl.pallas_call(
        _experts_body,
        grid_spec=grid_spec,
        out_shape=jax.ShapeDtypeStruct((n_rows, d), F32),
        compiler_params=_cparams(("arbitrary",)),
        name="experts",
    )(block_e, n_used, xs, wg, wu, wd)


def _combine_body(tm, pos_ref, posn_ref, ys_hbm, w_ref, h_ref, x1_ref, g2_ref, fg_ref,
                  sg_ref, su_ref, sd_ref, o_ref, buf_ref, sem_ref):
    i = pl.program_id(0)
    nt = pl.num_programs(0)
    slot = lax.rem(i, 2)
    nrow = TOP_K * tm

    def row_copy(src, r, s):
        return pltpu.make_async_copy(ys_hbm.at[pl.ds(src, 1)], buf_ref.at[s, pl.ds(r, 1)], sem_ref.at[s])

    def issue(idx_ref, s):
        def body(r, carry):
            row_copy(idx_ref[0, 0, r], r, s).start()
            return carry
        lax.fori_loop(0, nrow, body, 0, unroll=8)

    @pl.when(i == 0)
    def _():
        issue(pos_ref, 0)

    @pl.when(i + 1 < nt)
    def _():
        issue(posn_ref, 1 - slot)

    pltpu.make_async_copy(ys_hbm.at[pl.ds(0, nrow)], buf_ref.at[slot], sem_ref.at[slot]).wait()

    w = w_ref[...]
    routed = jnp.zeros(h_ref.shape, F32)
    for k in range(TOP_K):
        routed = routed + w[:, k:k + 1] * buf_ref[slot, k * tm:(k + 1) * tm, :]
    xb = h_ref[...].astype(BF16)
    hg = jnp.dot(xb, sg_ref[...], preferred_element_type=F32)
    hu = jnp.dot(xb, su_ref[...], preferred_element_type=F32)
    shared = jnp.dot((_silu(hg) * hu).astype(BF16), sd_ref[...], preferred_element_type=F32)
    x2 = x1_ref[...] + g2_ref[0] * (routed + shared)
    r = lax.rsqrt(jnp.mean(x2 * x2, axis=-1, keepdims=True) + NORM_EPS)
    o_ref[...] = x2 * r * fg_ref[...]


def _tile_slots(pos, tm):
    k, t = pos.shape
    return pos.reshape(k, t // tm, tm).transpose(1, 0, 2).reshape(t // tm, 1, k * tm)


def _combine(ys, pos_t, w6, h2, x1, gate2, final_g, sg, su, sd, tokens_per_batch, tm):
    t, d = h2.shape
    nt = t // tm
    tpb = tokens_per_batch // tm
    ds = sg.shape[1]
    wpad = jnp.zeros((t, LANES), F32).at[:, :TOP_K].set(w6.T)
    tile = pl.BlockSpec((tm, d), lambda i: (i, 0))
    return pl.pallas_call(
        functools.partial(_combine_body, tm),
        grid=(nt,),
        in_specs=[pl.BlockSpec((1, 1, TOP_K * tm), lambda i: (i, 0, 0), memory_space=pltpu.SMEM),
                  pl.BlockSpec((1, 1, TOP_K * tm), lambda i: (jnp.minimum(i + 1, nt - 1), 0, 0),
                               memory_space=pltpu.SMEM),
                  pl.BlockSpec(memory_space=pl.ANY),
                  pl.BlockSpec((tm, LANES), lambda i: (i, 0)),
                  tile, tile,
                  pl.BlockSpec((1, 1, d), lambda i: (i // tpb, 0, 0)),
                  pl.BlockSpec((1, d), lambda i: (0, 0)),
                  pl.BlockSpec((d, ds), lambda i: (0, 0)),
                  pl.BlockSpec((d, ds), lambda i: (0, 0)),
                  pl.BlockSpec((ds, d), lambda i: (0, 0))],
        out_specs=tile,
        out_shape=jax.ShapeDtypeStruct((t, d), F32),
        scratch_shapes=[pltpu.VMEM((2, TOP_K * tm, d), F32), pltpu.SemaphoreType.DMA((2,))],
        compiler_params=_cparams(("arbitrary",)),
        name="moe_combine",
    )(pos_t, pos_t, ys, wpad, h2, x1, gate2, final_g.reshape(1, d), sg, su, sd)


def _ssd_params(ssm_conv_w, ssm_conv_b, ssm_dt_bias, ssm_a_log, ssm_d, ssm_norm_g):
    dx = SSM_HEADS * SSM_HEAD_DIM
    q = SSD_CHUNK
    pad = lambda v: jnp.zeros((2, 1, LANES), F32).at[:, 0, :SSM_HEADS].set(v)
    heads = jnp.arange(SSM_HEADS)
    e1 = (jnp.arange(dx)[None, :] // SSM_HEAD_DIM == heads[:, None]).astype(F32)
    e2 = (jnp.arange(SSM_HEADS * q)[None, :] // q == heads[:, None]).astype(F32)

    def stack(e):
        z = jnp.zeros((2 * LANES, e.shape[1]), F32)
        return z.at[:SSM_HEADS].set(e).at[LANES:LANES + SSM_HEADS].set(e).astype(BF16)

    return dict(
        cwx=ssm_conv_w[:, :dx], cbx=ssm_conv_b[None, :dx],
        cwbc=ssm_conv_w[:, dx:], cbbc=ssm_conv_b[None, dx:],
        dtb=pad(ssm_dt_bias), a=pad(-jnp.exp(ssm_a_log.astype(F32))),
        dexp=jnp.repeat(ssm_d, SSM_HEAD_DIM)[None, :], ng=ssm_norm_g[None, :],
        e=stack(e1), e2=stack(e2))


def kernel(x, c, ctx, c_ctx, ada_w, ada_b, norm1_g, norm2_g, w_in, ssm_conv_w, ssm_conv_b, ssm_dt_bias, ssm_a_log, ssm_d, ssm_norm_g, w_ssm_out, hy_conv_w, hy_conv_b, hy_w1, hy_b1, hy_w2, hy_b2, hy_w3, hy_b3, hy_freq, hy_w_out, hy_decay, hy_bias, w_hy_out, w_o, router_w, router_bias, e_w_gate, e_w_up, e_w_down, sh_w_gate, sh_w_up, sh_w_down, final_g):
    depth = ada_w.shape[0]
    assert depth == 1, "single-layer block"
    b, l, d = x.shape
    lc = ctx.shape[1]
    dx = SSM_HEADS * SSM_HEAD_DIM
    dbc = 2 * SSM_GROUPS * SSM_STATE
    assert dx == d and hy_bias.shape[-1] == d

    rows = -(-(b + 1) // BF16_SUBLANES) * BF16_SUBLANES
    cc = jnp.zeros((rows, d), F32).at[:b].set(c).at[b].set(c_ctx)
    mod_all = _matmul(_silu(cc).astype(BF16), ada_w[0].astype(BF16), F32, rows, 512, "adaln") + ada_b[0]
    mod = mod_all[:b].reshape(b, 6, d)
    mod_c = jnp.broadcast_to(mod_all[b].reshape(1, 6, d), (b, 6, d))

    w = w_in[0]
    o_dt, o_z, o_hy, o_g = dx + dbc, dx + dbc + 2 * SSM_HEADS, 2 * dx + dbc + 2 * SSM_HEADS, 2 * dx + dbc + 2 * SSM_HEADS + 3 * d
    w_main = jnp.concatenate([w[:, o_hy:o_g], w[:, o_g:], w[:, o_z:o_hy], w[:, :dx + dbc]], axis=1).astype(BF16)
    w_dt = (jnp.zeros((d, 2 * LANES), F32)
            .at[:, :SSM_HEADS].set(w[:, o_dt:o_dt + SSM_HEADS])
            .at[:, LANES:LANES + SSM_HEADS].set(w[:, o_dt + SSM_HEADS:o_z])).astype(BF16)
    col_ghy, col_gssd, col_z, col_xs, col_bc = 3, 4, 5, 6, 7 * (d // dbc)

    sp = _ssd_params(ssm_conv_w[0], ssm_conv_b[0], ssm_dt_bias[0], ssm_a_log[0], ssm_d[0], ssm_norm_g[0])

    hc = _normmod(ctx, norm1_g[0], mod_c[:, 0], mod_c[:, 1])
    w_ctx = w_main[:, col_xs * d:]
    proj_c = _matmul(hc.reshape(b * lc, d), w_ctx, BF16, 1024, 512, "ctx_proj").reshape(b, lc, -1)
    dt_c = _matmul(hc.reshape(b * lc, d), w_dt, F32, 1024, 2 * LANES, "ctx_dt").reshape(b, lc, 2 * LANES)
    zero_state = jnp.zeros((b, 2, SSM_STATE, dx), F32)
    ctx_state = _ssd(proj_c, dt_c, zero_state, sp, False, 0, d // dbc, 0)

    hx = _normmod(x, norm1_g[0], mod[:, 0], mod[:, 1]).reshape(b * l, d)
    proj = _matmul(hx, w_main, BF16, 1024, 512, "in_proj").reshape(b, l, -1)
    dt_raw = _matmul(hx, w_dt, F32, 1024, 2 * LANES, "dt_proj").reshape(b, l, 2 * LANES)
    yn = _ssd(proj, dt_raw, ctx_state, sp, True, col_xs, col_bc, col_z)

    fb = min(l, 256)
    ct = 512
    wf, wi = _dft_tables(l, fb)
    hp = dict(hy_w1=hy_w1[0], hy_b1=hy_b1[0], hy_w2=hy_w2[0], hy_b2=hy_b2[0], hy_w3=hy_w3[0], hy_b3=hy_b3[0],
              hy_freq=hy_freq[0], hy_w_out=hy_w_out[0], hy_decay=hy_decay[0])
    coef = _filter_coefs(l, fb, ct, wf, hp)
    zhy = _hyena(proj, hy_conv_w[0], hy_conv_b[0][None, :], wf, wi, coef, hy_bias[0], ct, fb)

    x1, h2 = _outproj(zhy, yn, proj, x, mod[:, 2], mod[:, 3], mod[:, 4], norm2_g[0],
                      w_hy_out[0].astype(BF16), w_ssm_out[0].astype(BF16), w_o[0].astype(BF16),
                      col_ghy, col_gssd)
    x1 = x1.reshape(b * l, d)
    h2 = h2.reshape(b * l, d)

    e6, c6, w6, counts = _route(h2, router_w[0], router_bias[0])
    pos, block_e, n_used, zero_start, zero_n, nb = _dispatch_plan(e6, c6, counts, b * l * TOP_K)
    tm = min(l, 128)
    pos_t = _tile_slots(pos, tm)
    xs = _dispatch(h2, pos_t, zero_start, zero_n, n_used, nb, tm)
    ys = _experts(xs, block_e, n_used, e_w_gate[0].astype(BF16), e_w_up[0].astype(BF16),
                  e_w_down[0].astype(BF16))
    out = _combine(ys, pos_t, w6, h2, x1, mod[:, 5].reshape(b, 1, d), final_g,
                   sh_w_gate[0].astype(BF16), sh_w_up[0].astype(BF16), sh_w_down[0].astype(BF16), l, tm)
    return out.reshape(b, l, d)
```

```python
import functools
import math

import jax
import jax.numpy as jnp
from jax import lax
from jax.experimental import pallas as pl
from jax.experimental.pallas import tpu as pltpu

F32 = jnp.float32
BF16 = jnp.bfloat16

NORM_EPS = 1e-6
SSM_HEADS = 16
SSM_HEAD_DIM = 64
SSM_GROUPS = 2
SSM_STATE = 128
SSD_CHUNK = 128
HY_ORDER = 2
HY_BANDS = 16
N_EXPERTS = 64
TOP_K = 6
N_EXPERT_GROUPS = 8
TOPK_GROUPS = 4
ROUTED_SCALE = 2.5
EXPERT_BLOCK = 256

LANES = 128
BF16_SUBLANES = 16
VMEM_LIMIT = 56 * 1024 * 1024


def _cparams(sem):
    return pltpu.CompilerParams(dimension_semantics=sem, vmem_limit_bytes=VMEM_LIMIT)


def _silu(x):
    return x * jax.nn.sigmoid(x)


def _split_bf16(q):
    hi = q.astype(BF16)
    lo = (q - hi.astype(F32)).astype(BF16)
    return hi, lo


def _normmod_body(x_ref, g_ref, sh_ref, sc_ref, o_ref):
    x = x_ref[0]
    r = lax.rsqrt(jnp.mean(x * x, axis=-1, keepdims=True) + NORM_EPS)
    y = x * r * g_ref[...]
    o_ref[...] = (y * (1.0 + sc_ref[0]) + sh_ref[0]).astype(o_ref.dtype)


def _normmod(x, g, shift, scale, out_dtype=BF16):
    b, l, d = x.shape
    tl = min(l, 512)
    nl = l // tl
    return pl.pallas_call(
        _normmod_body,
        grid=(b, nl),
        in_specs=[pl.BlockSpec((1, tl, d), lambda i, j: (i, j, 0)),
                  pl.BlockSpec((1, d), lambda i, j: (0, 0)),
                  pl.BlockSpec((1, 1, d), lambda i, j: (i, 0, 0)),
                  pl.BlockSpec((1, 1, d), lambda i, j: (i, 0, 0))],
        out_specs=pl.BlockSpec((tl, d), lambda i, j: (i * nl + j, 0)),
        out_shape=jax.ShapeDtypeStruct((b * l, d), out_dtype),
        compiler_params=_cparams(("parallel", "parallel")),
        name="normmod",
    )(x, g.reshape(1, d), shift.reshape(b, 1, d), scale.reshape(b, 1, d))


def _mm_body(a_ref, w_ref, o_ref):
    o_ref[...] = jnp.dot(a_ref[...], w_ref[...], preferred_element_type=F32).astype(o_ref.dtype)


def _matmul(a, w, out_dtype, tm, tn, name):
    m, k = a.shape
    n = w.shape[1]
    tm, tn = min(tm, m), min(tn, n)
    assert m % tm == 0 and n % tn == 0, (m, n, tm, tn)
    return pl.pallas_call(
        _mm_body,
        grid=(m // tm, n // tn),
        in_specs=[pl.BlockSpec((tm, k), lambda i, j: (i, 0)),
                  pl.BlockSpec((k, tn), lambda i, j: (0, j))],
        out_specs=pl.BlockSpec((tm, tn), lambda i, j: (i, j)),
        out_shape=jax.ShapeDtypeStruct((m, n), out_dtype),
        compiler_params=_cparams(("parallel", "arbitrary")),
        name=name,
    )(a, w)


def _conv3_roll(u, w_ref, b_ref):
    q = u.shape[0]
    row = lax.broadcasted_iota(jnp.int32, u.shape, 0)
    um1 = jnp.where(row == 0, 0.0, pltpu.roll(u, 1, 0))
    up1 = jnp.where(row == q - 1, 0.0, pltpu.roll(u, q - 1, 0))
    return um1 * w_ref[0:1, :] + u * w_ref[1:2, :] + up1 * w_ref[2:3, :] + b_ref[...]


def _conv3(ext, off, rows, w_ref, b_ref, keep_prev=1, keep_next=1):
    n = ext.shape[0]
    r = lax.broadcasted_iota(jnp.int32, (rows, n), 0)
    c = lax.broadcasted_iota(jnp.int32, (rows, n), 1)
    down = ((c == r + (off - 1)) & (r + keep_prev > 0)).astype(BF16)
    up = ((c == r + (off + 1)) & (r - keep_next < rows - 1)).astype(BF16)
    um1 = jnp.dot(down, ext, preferred_element_type=F32)
    up1 = jnp.dot(up, ext, preferred_element_type=F32)
    u = ext[off:off + rows].astype(F32)
    return um1 * w_ref[0:1, :] + u * w_ref[1:2, :] + up1 * w_ref[2:3, :] + b_ref[...]


def _ssd_body(nc, emit_y, xs_ref, xsp_ref, xsn_ref, bc_ref, bcp_ref, bcn_ref, dt_ref, z_ref,
              cwx_ref, cbx_ref, cwbc_ref, cbbc_ref, dtb_ref, a_ref, dexp_ref, ng_ref,
              e_ref, e2_ref, init_ref, o_ref, st_ref, ysc_ref):
    q = SSD_CHUNK
    p = pl.program_id(1)
    c = pl.program_id(2)
    is_fwd = p == 1
    ci = jnp.where(is_fwd, c, nc - 1 - c)

    @pl.when(c == 0)
    def _():
        st_ref[...] = init_ref[0, 0]

    keep_prev = (ci > 0).astype(jnp.int32)
    keep_next = (ci < nc - 1).astype(jnp.int32)
    halo = BF16_SUBLANES
    xs = _silu(_conv3(jnp.concatenate([xsp_ref[...], xs_ref[...], xsn_ref[...]], axis=0), halo, q,
                      cwx_ref, cbx_ref, keep_prev, keep_next))
    bc = _silu(_conv3(jnp.concatenate([bcp_ref[...], bc_ref[...], bcn_ref[...]], axis=0), halo, q,
                      cwbc_ref, cbbc_ref, keep_prev, keep_next))
    ng2 = SSM_GROUPS * SSM_STATE
    bm, cm = bc[:, :ng2], bc[:, ng2:]

    dtr = dt_ref[...] + dtb_ref[0]
    dt = jnp.maximum(dtr, 0.0) + jnp.log(1.0 + jnp.exp(-jnp.abs(dtr)))
    a = dt * a_ref[0]

    rr = lax.broadcasted_iota(jnp.int32, (q, q), 0)
    cc = lax.broadcasted_iota(jnp.int32, (q, q), 1)
    mask = (rr - cc) * jnp.where(is_fwd, 1, -1) >= 0
    tri = mask.astype(BF16)
    a_hi, a_lo = _split_bf16(a)
    acs2 = jnp.dot(tri, jnp.concatenate([a_hi, a_lo], axis=1), preferred_element_type=F32)
    acs = acs2[:, :LANES] + acs2[:, LANES:]
    c_hi, c_lo = _split_bf16(acs)
    acs_cat = jnp.concatenate([c_hi, c_lo], axis=1)
    ex = jnp.dot(acs_cat, e_ref[...], preferred_element_type=F32)
    d_hi, d_lo = _split_bf16(dt)
    dt_exp = jnp.dot(jnp.concatenate([d_hi, d_lo], axis=1), e_ref[...], preferred_element_type=F32)
    ex_end = jnp.where(is_fwd, ex[q - 1:q, :], ex[0:1, :])

    xdt = xs * dt_exp
    xw = (xdt * jnp.exp(ex_end - ex)).astype(BF16)
    state = st_ref[...]

    if emit_y:
        acs_t = acs.T
        cb_all = jnp.dot(acs_cat, e2_ref[...], preferred_element_type=F32)
        lane = lax.broadcasted_iota(jnp.int32, (q, LANES), 1)
        state_bf = state.astype(BF16)
        hpg = SSM_HEADS // SSM_GROUPS
        gw = hpg * SSM_HEAD_DIM
        pieces = []
        for g in range(SSM_GROUPS):
            cg = cm[:, g * SSM_STATE:(g + 1) * SSM_STATE].astype(BF16)
            bg = bm[:, g * SSM_STATE:(g + 1) * SSM_STATE].astype(BF16)
            cbg = lax.dot_general(cg, bg, (((1,), (1,)), ((), ())), preferred_element_type=F32)
            yoff = jnp.dot(cg, state_bf[:, g * gw:(g + 1) * gw], preferred_element_type=F32)
            yoff = yoff * jnp.exp(ex[:, g * gw:(g + 1) * gw])
            for j in range(hpg // 2):
                h0 = g * hpg + 2 * j
                ms = []
                for h in (h0, h0 + 1):
                    seg = cb_all[:, h * q:(h + 1) * q] - acs_t[h:h + 1, :]
                    ms.append((cbg * jnp.exp(jnp.where(mask, seg, -jnp.inf))).astype(BF16))
                m2 = jnp.concatenate(ms, axis=1)
                xp = xdt[:, h0 * SSM_HEAD_DIM:(h0 + 2) * SSM_HEAD_DIM]
                x2 = jnp.concatenate([jnp.where(lane < SSM_HEAD_DIM, xp, 0.0),
                                      jnp.where(lane >= SSM_HEAD_DIM, xp, 0.0)], axis=0).astype(BF16)
                yd = jnp.dot(m2, x2, preferred_element_type=F32)
                pieces.append(yd + yoff[:, 2 * j * SSM_HEAD_DIM:(2 * j + 2) * SSM_HEAD_DIM])
        y = jnp.concatenate(pieces, axis=1)
        row0 = pl.multiple_of(ci * q, q)

        @pl.when(p == 0)
        def _():
            ysc_ref[pl.ds(row0, q), :] = y

        @pl.when(p == 1)
        def _():
            yt = y + ysc_ref[pl.ds(row0, q), :] + xs * dexp_ref[...]
            v = yt * _silu(z_ref[...].astype(F32))
            outs = []
            for g in range(SSM_GROUPS):
                vg = v[:, g * gw:(g + 1) * gw]
                outs.append(vg * lax.rsqrt(jnp.mean(vg * vg, axis=-1, keepdims=True) + NORM_EPS))
            o_ref[...] = (jnp.concatenate(outs, axis=1) * ng_ref[...]).astype(o_ref.dtype)

    new_parts = []
    hpg = SSM_HEADS // SSM_GROUPS
    gw = hpg * SSM_HEAD_DIM
    for g in range(SSM_GROUPS):
        bt = bm[:, g * SSM_STATE:(g + 1) * SSM_STATE].T.astype(BF16)
        new_parts.append(jnp.dot(bt, xw[:, g * gw:(g + 1) * gw], preferred_element_type=F32))
    st_new = state * jnp.exp(ex_end) + jnp.concatenate(new_parts, axis=1)
    st_ref[...] = st_new
    if not emit_y:
        @pl.when(c == nc - 1)
        def _():
            o_ref[0, 0] = st_new


def _ssd(proj, dt_raw, init, prm, emit_y, xs_col, bc_col, z_col):
    b = init.shape[0]
    l = proj.shape[0] // b
    q = SSD_CHUNK
    nc = l // q
    hb = q // BF16_SUBLANES
    nh = l // BF16_SUBLANES
    dx = SSM_HEADS * SSM_HEAD_DIM
    dbc = 2 * SSM_GROUPS * SSM_STATE

    def ci_of(p, c):
        return p * c + (1 - p) * (nc - 1 - c)

    def main(col):
        return lambda i, p, c: (i * nc + ci_of(p, c), col)

    def prev(col):
        return lambda i, p, c: (i * nh + jnp.maximum(ci_of(p, c) * hb - 1, 0), col)

    def nxt(col):
        return lambda i, p, c: (i * nh + jnp.minimum(ci_of(p, c) * hb + hb, nh - 1), col)

    const2 = lambda i, p, c: (0, 0)
    bydir = lambda i, p, c: (1 - p, 0, 0)
    in_specs = [
        pl.BlockSpec((q, dx), main(xs_col)),
        pl.BlockSpec((BF16_SUBLANES, dx), prev(xs_col)),
        pl.BlockSpec((BF16_SUBLANES, dx), nxt(xs_col)),
        pl.BlockSpec((q, dbc), main(bc_col)),
        pl.BlockSpec((BF16_SUBLANES, dbc), prev(bc_col)),
        pl.BlockSpec((BF16_SUBLANES, dbc), nxt(bc_col)),
        pl.BlockSpec((q, LANES), lambda i, p, c: (i * nc + ci_of(p, c), 1 - p)),
        pl.BlockSpec((q, dx), (lambda i, p, c: (i * nc + c * p, z_col))),
        pl.BlockSpec((3, dx), const2), pl.BlockSpec((1, dx), const2),
        pl.BlockSpec((3, dbc), const2), pl.BlockSpec((1, dbc), const2),
        pl.BlockSpec((1, 1, LANES), bydir), pl.BlockSpec((1, 1, LANES), bydir),
        pl.BlockSpec((1, dx), const2), pl.BlockSpec((1, dx), const2),
        pl.BlockSpec((2 * LANES, dx), const2),
        pl.BlockSpec((2 * LANES, SSM_HEADS * q), const2),
        pl.BlockSpec((1, 1, SSM_STATE, dx), lambda i, p, c: (i, 1 - p, 0, 0)),
    ]
    if emit_y:
        out_spec = pl.BlockSpec((q, dx), lambda i, p, c: (i * nc + c * p, 0))
        out_shape = jax.ShapeDtypeStruct((b * l, dx), BF16)
    else:
        out_spec = pl.BlockSpec((1, 1, SSM_STATE, dx), lambda i, p, c: (i, 1 - p, 0, 0))
        out_shape = jax.ShapeDtypeStruct((b, 2, SSM_STATE, dx), F32)
    return pl.pallas_call(
        functools.partial(_ssd_body, nc, emit_y),
        grid=(b, 2, nc),
        in_specs=in_specs,
        out_specs=out_spec,
        out_shape=out_shape,
        scratch_shapes=[pltpu.VMEM((SSM_STATE, dx), F32),
                        pltpu.VMEM((l if emit_y else q, dx), F32)],
        compiler_params=_cparams(("parallel", "arbitrary", "arbitrary")),
        name="ssd_scan" if emit_y else "ssd_ctx_state",
    )(proj, proj, proj, proj, proj, proj, dt_raw, proj,
      prm["cwx"], prm["cbx"], prm["cwbc"], prm["cbbc"], prm["dtb"], prm["a"], prm["dexp"], prm["ng"],
      prm["e"], prm["e2"], init)


def _hyena_body(nfb, fb, x1_ref, x2_ref, v_ref, cw1_ref, cw2_ref, cw3_ref, cb1_ref, cb2_ref, cb3_ref,
                we_ref, wo_ref, wie_ref, wio_ref, coef_ref, skip_ref, kmid_ref, o_ref,
                zf_ref, ze_ref, zo_ref, acce_ref, acco_ref, lc_ref, x1c_ref, x2c_ref, mid_ref):
    o = pl.program_id(2)
    k = pl.program_id(3)
    ns, l, _ = zf_ref.shape
    ct = ns * LANES
    h = l // 2
    slab = lambda j: slice(j * LANES, (j + 1) * LANES)

    def split_z():
        kre = jnp.where(o == 0, kmid_ref[0:1, :], kmid_ref[2:3, :])
        kim = jnp.where(o == 0, kmid_ref[1:2, :], kmid_ref[3:4, :])
        j4 = lax.broadcasted_iota(jnp.int32, (8, LANES), 0) & 3
        for j in range(ns):
            ze_ref[:, slab(j)] = zf_ref[j, pl.ds(0, h, stride=2), :].astype(BF16)
            zo_ref[:, slab(j)] = zf_ref[j, pl.ds(1, h, stride=2), :].astype(BF16)
            z8 = jnp.sum(zf_ref[j].reshape(l // 8, 8, LANES), axis=0)
            ure = z8[0:1] + z8[4:5] - z8[2:3] - z8[6:7]
            uim = z8[3:4] + z8[7:8] - z8[1:2] - z8[5:6]
            yre = ure * kre[:, slab(j)] - uim * kim[:, slab(j)]
            yim = ure * kim[:, slab(j)] + uim * kre[:, slab(j)]
            pat = jnp.where(j4 == 0, yre, jnp.where(j4 == 1, -yim, jnp.where(j4 == 2, -yre, yim)))
            mid_ref[:, slab(j)] = pat * (1.0 / l)

    @pl.when((o == 0) & (k == 0))
    def _():
        rows = min(l, 256)
        halo = BF16_SUBLANES
        for r0 in range(0, l, rows):
            lo, hi = max(r0 - halo, 0), min(r0 + rows + halo, l)
            mid = slice(r0 - lo, r0 - lo + rows)
            x1c_ref[r0:r0 + rows, :] = _conv3_roll(x1_ref[lo:hi, :].astype(F32), cw1_ref, cb1_ref)[mid].astype(BF16)
            x2c_ref[r0:r0 + rows, :] = _conv3_roll(x2_ref[lo:hi, :].astype(F32), cw2_ref, cb2_ref)[mid].astype(BF16)
            vc = _conv3_roll(v_ref[lo:hi, :].astype(F32), cw3_ref, cb3_ref)[mid]
            for j in range(ns):
                zf_ref[j, r0:r0 + rows, :] = vc[:, slab(j)]
        split_z()

    @pl.when(k == 0)
    def _():
        acce_ref[...] = jnp.zeros_like(acce_ref)
        acco_ref[...] = jnp.zeros_like(acco_ref)

    a = jnp.dot(we_ref[...], ze_ref[...], preferred_element_type=F32)
    b = jnp.dot(wo_ref[...], zo_ref[...], preferred_element_type=F32)
    lre, lim = a[:fb] + b[:fb], a[fb:] + b[fb:]
    hre, him = a[:fb] - b[:fb], b[fb:] - a[fb:]
    c = coef_ref[0, 0, 0]
    ylre, ylim = lre * c[0] - lim * c[1], lre * c[1] + lim * c[0]
    yhre, yhim = hre * c[2] - him * c[3], hre * c[3] + him * c[2]
    ge = jnp.concatenate([ylre + yhre, ylim - yhim], axis=0).astype(BF16)
    go = jnp.concatenate([ylre - yhre, ylim + yhim], axis=0).astype(BF16)
    acce_ref[...] += jnp.dot(wie_ref[0], ge, preferred_element_type=F32)
    acco_ref[...] += jnp.dot(wio_ref[0], go, preferred_element_type=F32)

    @pl.when(k == nfb - 1)
    def _():
        skip = jnp.where(o == 0, skip_ref[0:1, :], skip_ref[1:2, :])

        def long_conv(j):
            lc_ref[j, pl.ds(0, h, stride=2), :] = acce_ref[:, slab(j)]
            lc_ref[j, pl.ds(1, h, stride=2), :] = acco_ref[:, slab(j)]
            lc = lc_ref[j] + zf_ref[j] * skip[:, slab(j)]
            return (lc.reshape(l // 8, 8, LANES) + mid_ref[:, slab(j)][None]).reshape(l, LANES)

        @pl.when(o == 0)
        def _():
            for j in range(ns):
                zf_ref[j] = x1c_ref[:, slab(j)].astype(F32) * long_conv(j)
            split_z()

        @pl.when(o == 1)
        def _():
            for j in range(ns):
                o_ref[:, slab(j)] = (x2c_ref[:, slab(j)].astype(F32) * long_conv(j)).astype(o_ref.dtype)


def _hyena(proj, b, cw, cb, tables, coef, kmid, skip, ct, fb):
    l = proj.shape[0] // b
    d = skip.shape[1]
    h = l // 2
    nct = d // ct
    nfb = h // fb
    we, wo, wie, wio = tables

    def col(part):
        return lambda j, i, o, k: (i, part * nct + j)

    def wcol(part):
        return lambda j, i, o, k: (0, part * nct + j)

    return pl.pallas_call(
        functools.partial(_hyena_body, nfb, fb),
        grid=(nct, b, HY_ORDER, nfb),
        in_specs=[pl.BlockSpec((l, ct), col(0)), pl.BlockSpec((l, ct), col(1)), pl.BlockSpec((l, ct), col(2)),
                  pl.BlockSpec((3, ct), wcol(0)), pl.BlockSpec((3, ct), wcol(1)), pl.BlockSpec((3, ct), wcol(2)),
                  pl.BlockSpec((1, ct), wcol(0)), pl.BlockSpec((1, ct), wcol(1)), pl.BlockSpec((1, ct), wcol(2)),
                  pl.BlockSpec((2 * fb, h), lambda j, i, o, k: (k, 0)),
                  pl.BlockSpec((2 * fb, h), lambda j, i, o, k: (k, 0)),
                  pl.BlockSpec((1, h, 2 * fb), lambda j, i, o, k: (k, 0, 0)),
                  pl.BlockSpec((1, h, 2 * fb), lambda j, i, o, k: (k, 0, 0)),
                  pl.BlockSpec((1, 1, 1, 4, fb, ct), lambda j, i, o, k: (o, j, k, 0, 0, 0)),
                  pl.BlockSpec((HY_ORDER, ct), lambda j, i, o, k: (0, j)),
                  pl.BlockSpec((2 * HY_ORDER, ct), lambda j, i, o, k: (0, j))],
        out_specs=pl.BlockSpec((l, ct), lambda j, i, o, k: (i, j)),
        out_shape=jax.ShapeDtypeStruct((b * l, d), BF16),
        scratch_shapes=[pltpu.VMEM((ct // LANES, l, LANES), F32), pltpu.VMEM((h, ct), BF16),
                        pltpu.VMEM((h, ct), BF16), pltpu.VMEM((h, ct), F32), pltpu.VMEM((h, ct), F32),
                        pltpu.VMEM((ct // LANES, l, LANES), F32),
                        pltpu.VMEM((l, ct), BF16), pltpu.VMEM((l, ct), BF16), pltpu.VMEM((8, ct), F32)],
        compiler_params=_cparams(("parallel", "parallel", "arbitrary", "arbitrary")),
        name="hyena",
    )(proj, proj, proj, cw, cw, cw, cb, cb, cb, we, wo, wie, wio, coef, skip, kmid)


def _cis_table(rows, cols, n):
    t = jnp.arange(cols, dtype=jnp.int32)
    digit = 16

    def cis(mult, count):
        j = jnp.arange(count, dtype=jnp.int32)
        ang = (2.0 * math.pi / n) * ((j[:, None] * mult * t[None, :]) % n).astype(F32)
        return jnp.cos(ang), jnp.sin(ang)

    def cmul(a, b):
        (ca, sa), (cb_, sb) = a, b
        c = ca[:, None] * cb_[None] - sa[:, None] * sb[None]
        s = sa[:, None] * cb_[None] + ca[:, None] * sb[None]
        return c.reshape(-1, cols), s.reshape(-1, cols)

    c, s = cmul(cis(digit * digit, -(-rows // (digit * digit))), cmul(cis(digit, digit), cis(1, digit)))
    return c[:rows], s[:rows]


def _dft_tables(l, fb):
    n, h = 2 * l, l // 2
    f = jnp.arange(h, dtype=F32)[:, None]
    ce, se = _cis_table(h, h, l)
    ct_, st_ = jnp.cos((2.0 * math.pi / n) * f), jnp.sin((2.0 * math.pi / n) * f)
    co, so = ce * ct_ - se * st_, se * ct_ + ce * st_
    cf = jnp.where(f == 0, 1.0 / n, 2.0 / n)

    def blocked(re, im):
        return jnp.concatenate([re.reshape(h // fb, 1, fb, h), im.reshape(h // fb, 1, fb, h)], axis=1).reshape(l, h)

    def inverse(c, s):
        return blocked(cf * c, -(2.0 / n) * s).reshape(h // fb, 2 * fb, h).transpose(0, 2, 1).astype(BF16)

    return (blocked(ce, -se).astype(BF16), blocked(co, -so).astype(BF16), inverse(ce, se), inverse(co, so))


def _implicit_filters(l, hp):
    hi = lax.Precision.HIGHEST
    pos = jnp.arange(l, dtype=F32)
    t = pos / max(l - 1, 1)
    bands = jnp.linspace(1e-4, HY_BANDS - 1, HY_BANDS, dtype=F32)
    ang = (2.0 * math.pi / l) * pos[:, None] * bands[None, :]
    z = jnp.concatenate([t[:, None], jnp.cos(ang), -jnp.sin(ang)], axis=-1)
    freq = hp["hy_freq"]
    hdn = jnp.sin(freq[0] * (jnp.dot(z, hp["hy_w1"], precision=hi) + hp["hy_b1"]))
    hdn = jnp.sin(freq[1] * (jnp.dot(hdn, hp["hy_w2"], precision=hi) + hp["hy_b2"]))
    hdn = jnp.sin(freq[2] * (jnp.dot(hdn, hp["hy_w3"], precision=hi) + hp["hy_b3"]))
    d = hp["hy_decay"].shape[-1]
    h = jnp.dot(hdn, hp["hy_w_out"], precision=hi).reshape(l, 2, HY_ORDER, d)
    window = jnp.exp(-t[:, None, None] * jnp.abs(hp["hy_decay"]))
    return h * window[:, None]


def _filter_coefs(l, fb, ct, we, wo, hp):
    hfl = _implicit_filters(l, hp)
    d = hfl.shape[-1]
    h = l // 2
    hf = hfl[:, 0]
    hb = hfl[:, 1].at[0].set(0.0)
    hcat = jnp.concatenate([hf.reshape(l, HY_ORDER * d), hb.reshape(l, HY_ORDER * d)], axis=1)

    def parts(s):
        return jnp.moveaxis(s.reshape(h // fb, 2, fb, 2, HY_ORDER, d), 1, 0).reshape(2, h, 2, HY_ORDER, d)

    a = parts(_matmul(we, hcat[0::2].astype(BF16), F32, 512, 512, "filter_dft_even"))
    b = parts(_matmul(wo, hcat[1::2].astype(BF16), F32, 512, 512, "filter_dft_odd"))
    lo_re, lo_im = a[0] + b[0], a[1] + b[1]
    hi_re, hi_im = a[0] - b[0], b[1] - a[1]
    coef = jnp.stack([lo_re[:, 0] + lo_re[:, 1], lo_im[:, 0] - lo_im[:, 1],
                      hi_re[:, 0] + hi_re[:, 1], hi_im[:, 0] - hi_im[:, 1]], axis=0)
    coef = coef.reshape(4, h // fb, fb, HY_ORDER, d // ct, ct).transpose(3, 4, 1, 0, 2, 5)

    t4 = jnp.arange(l) % 4
    pat = jnp.stack([jnp.where(t4 == 0, 1.0, jnp.where(t4 == 2, -1.0, 0.0)),
                     jnp.where(t4 == 3, 1.0, jnp.where(t4 == 1, -1.0, 0.0))]).astype(F32)
    fm = jnp.dot(pat, hcat, precision=lax.Precision.HIGHEST).reshape(2, 2, HY_ORDER, d)
    kmid = jnp.stack([fm[0, 0] + fm[0, 1], fm[1, 0] - fm[1, 1]], axis=1).reshape(2 * HY_ORDER, d)
    return coef, kmid


def _outproj_body(zhy_ref, yn_ref, ghy_ref, gssd_ref, x_ref, g1_ref, sh2_ref, sc2_ref, n2g_ref,
                  why_ref, wssm_ref, wo_ref, x1_ref, h2_ref):
    hyo = jnp.dot(zhy_ref[...], why_ref[...], preferred_element_type=F32)
    sso = jnp.dot(yn_ref[...], wssm_ref[...], preferred_element_type=F32)
    merged = (jax.nn.sigmoid(ghy_ref[...].astype(F32)) * hyo
              + jax.nn.sigmoid(gssd_ref[...].astype(F32)) * sso)
    xmix = jnp.dot(merged.astype(BF16), wo_ref[...], preferred_element_type=F32)
    x1 = x_ref[0] + g1_ref[0] * xmix
    x1_ref[...] = x1
    r = lax.rsqrt(jnp.mean(x1 * x1, axis=-1, keepdims=True) + NORM_EPS)
    h2_ref[...] = (x1 * r * n2g_ref[...]) * (1.0 + sc2_ref[0]) + sh2_ref[0]


def _outproj(zhy, yn, proj, x, gate1, shift2, scale2, n2g, why, wssm, wo, ghy_col, gssd_col):
    b, l, d = x.shape
    tm = min(l, 512)
    nl = l // tm
    tile = lambda col: pl.BlockSpec((tm, d), lambda i, j: (i * nl + j, col))
    row = pl.BlockSpec((1, 1, d), lambda i, j: (i, 0, 0))
    wspec = pl.BlockSpec((d, d), lambda i, j: (0, 0))
    return pl.pallas_call(
        _outproj_body,
        grid=(b, nl),
        in_specs=[tile(0), tile(0), tile(ghy_col), tile(gssd_col),
                  pl.BlockSpec((1, tm, d), lambda i, j: (i, j, 0)), row, row, row,
                  pl.BlockSpec((1, d), lambda i, j: (0, 0)), wspec, wspec, wspec],
        out_specs=[tile(0), tile(0)],
        out_shape=[jax.ShapeDtypeStruct((b * l, d), F32), jax.ShapeDtypeStruct((b * l, d), F32)],
        compiler_params=_cparams(("parallel", "parallel")),
        name="merge_outproj",
    )(zhy, yn, proj, proj, x, gate1.reshape(b, 1, d), shift2.reshape(b, 1, d), scale2.reshape(b, 1, d),
      n2g.reshape(1, d), why, wssm, wo)


PER_GROUP = N_EXPERTS // N_EXPERT_GROUPS


def _route_body(tm, h_ref, wh_ref, wl_ref, bias_ref, e_ref, c_ref, w_ref, cnt_ref, carry_ref):
    ng, pg = N_EXPERT_GROUPS, PER_GROUP

    @pl.when(pl.program_id(0) == 0)
    def _():
        carry_ref[...] = jnp.zeros_like(carry_ref)

    x_hi, x_lo = _split_bf16(h_ref[...])
    nt = (((1,), (1,)), ((), ()))
    lg = (lax.dot_general(wh_ref[...], x_hi, nt, preferred_element_type=F32)
          + lax.dot_general(wh_ref[...], x_lo, nt, preferred_element_type=F32)
          + lax.dot_general(wl_ref[...], x_hi, nt, preferred_element_type=F32))
    sc = jax.nn.sigmoid(lg)
    sel = sc + bias_ref[...]

    mem = [sel[m * ng:(m + 1) * ng, :] for m in range(pg)]
    m1 = functools.reduce(jnp.maximum, mem)
    first = functools.reduce(jnp.minimum, [jnp.where(mem[m] == m1, m, pg) for m in range(pg)])
    m2 = functools.reduce(jnp.maximum, [jnp.where(first == m, -jnp.inf, mem[m]) for m in range(pg)])
    gs = m1 + m2
    gid = lax.broadcasted_iota(jnp.int32, gs.shape, 0)
    beats = jnp.zeros(gs.shape, jnp.int32)
    for g2 in range(ng):
        v = gs[g2:g2 + 1, :]
        beats = beats + ((v > gs) | ((v == gs) & (g2 < gid))).astype(jnp.int32)
    keep = beats < TOPK_GROUPS
    selm = jnp.concatenate([jnp.where(keep, mem[m], -jnp.inf) for m in range(pg)], axis=0)

    r = lax.broadcasted_iota(jnp.int32, selm.shape, 0)
    eid = (r % ng) * pg + r // ng
    rank = jnp.zeros(selm.shape, jnp.int32)
    for r2 in range(N_EXPERTS):
        v = selm[r2:r2 + 1, :]
        e2 = (r2 % ng) * pg + r2 // ng
        rank = rank + ((v > selm) | ((v == selm) & (e2 < eid))).astype(jnp.int32)
    chosen = rank < TOP_K
    wsel = jnp.where(chosen, sc, 0.0)
    wn = wsel / jnp.sum(wsel, axis=0, keepdims=True) * ROUTED_SCALE

    cb = chosen.astype(BF16)
    t0 = lax.broadcasted_iota(jnp.int32, (tm, tm), 0)
    t1 = lax.broadcasted_iota(jnp.int32, (tm, tm), 1)
    before = jnp.dot(cb, (t0 < t1).astype(BF16), preferred_element_type=F32) + carry_ref[...]
    carry = carry_ref[...] + jnp.dot(cb, jnp.ones((tm, tm), BF16), preferred_element_type=F32)
    carry_ref[...] = carry
    cnt_ref[...] = carry

    k8 = lax.broadcasted_iota(jnp.int32, e_ref.shape, 0)
    eidf = eid.astype(F32)
    e_out = jnp.zeros(e_ref.shape, F32)
    c_out = jnp.zeros(e_ref.shape, F32)
    w_out = jnp.zeros(e_ref.shape, F32)
    for k in range(TOP_K):
        mk = rank == k
        e_out = jnp.where(k8 == k, jnp.sum(jnp.where(mk, eidf, 0.0), axis=0, keepdims=True), e_out)
        c_out = jnp.where(k8 == k, jnp.sum(jnp.where(mk, before, 0.0), axis=0, keepdims=True), c_out)
        w_out = jnp.where(k8 == k, jnp.sum(jnp.where(mk, wn, 0.0), axis=0, keepdims=True), w_out)
    e_ref[...] = e_out.astype(jnp.int32)
    c_ref[...] = c_out.astype(jnp.int32)
    w_ref[...] = w_out


def _route(h2, router_w, router_bias):
    t, d = h2.shape
    tm = min(t, 256)
    ng, pg = N_EXPERT_GROUPS, PER_GROUP
    wt = router_w.T.reshape(ng, pg, d).transpose(1, 0, 2).reshape(N_EXPERTS, d)
    w_hi, w_lo = _split_bf16(wt)
    bias = jnp.broadcast_to(router_bias.astype(F32).reshape(ng, pg).T.reshape(N_EXPERTS, 1), (N_EXPERTS, tm))
    slot = pl.BlockSpec((8, tm), lambda i: (0, i))
    const = lambda shape: pl.BlockSpec(shape, lambda i: (0, 0))
    e8, c8, w8, cnt = pl.pallas_call(
        functools.partial(_route_body, tm),
        grid=(t // tm,),
        in_specs=[pl.BlockSpec((tm, d), lambda i: (i, 0)), const((N_EXPERTS, d)), const((N_EXPERTS, d)),
                  const((N_EXPERTS, tm))],
        out_specs=[slot, slot, slot, const((N_EXPERTS, tm))],
        out_shape=[jax.ShapeDtypeStruct((8, t), jnp.int32), jax.ShapeDtypeStruct((8, t), jnp.int32),
                   jax.ShapeDtypeStruct((8, t), F32), jax.ShapeDtypeStruct((N_EXPERTS, tm), F32)],
        scratch_shapes=[pltpu.VMEM((N_EXPERTS, tm), F32)],
        compiler_params=_cparams(("arbitrary",)),
        name="moe_route",
    )(h2, w_hi, w_lo, bias)
    counts = cnt[:, 0].reshape(pg, ng).T.reshape(N_EXPERTS).astype(jnp.int32)
    return e8[:TOP_K], c8[:TOP_K], w8[:TOP_K], counts


def _dispatch_plan(e6, c6, counts, n_slots):
    padded = (counts + EXPERT_BLOCK - 1) // EXPERT_BLOCK * EXPERT_BLOCK
    pad_end = jnp.cumsum(padded)
    pad_start = pad_end - padded
    nb = -(-n_slots // EXPERT_BLOCK) + N_EXPERTS
    blk = jnp.arange(nb, dtype=jnp.int32) * EXPERT_BLOCK
    block_e = jnp.minimum(jnp.sum((pad_end[None, :] <= blk[:, None]).astype(jnp.int32), axis=1), N_EXPERTS - 1)
    n_used = (pad_end[-1] // EXPERT_BLOCK).astype(jnp.int32).reshape(1)
    onehot = e6[..., None] == jnp.arange(N_EXPERTS, dtype=jnp.int32)
    pos = jnp.sum(jnp.where(onehot, pad_start, 0), axis=-1) + c6
    return pos.astype(jnp.int32), block_e.astype(jnp.int32), n_used, (pad_start + counts).astype(jnp.int32), \
        (padded - counts).astype(jnp.int32), nb


def _dispatch_body(tm, nt, nb, zs_ref, zn_ref, nu_ref, pos_ref, h_hbm, xs_hbm, buf_ref, zero_ref,
                   lsem_ref, sem_ref, zsem_ref):
    i = pl.program_id(0)
    slot = lax.rem(i, 3)
    nrow = TOP_K * tm

    def load(j, s):
        return pltpu.make_async_copy(h_hbm.at[pl.ds(pl.multiple_of(j * tm, tm), tm)], buf_ref.at[s], lsem_ref.at[s])

    @pl.when(i == 0)
    def _():
        load(0, 0).start()
        if nt > 1:
            load(1, 1).start()

    load(i, slot).wait()

    def body(r, carry):
        for k in range(TOP_K):
            pltpu.make_async_copy(buf_ref.at[slot, pl.ds(r, 1)],
                                  xs_hbm.at[pl.ds(pos_ref[0, 0, k * tm + r], 1)],
                                  sem_ref.at[slot]).start(priority=k % 2)
        return carry
    lax.fori_loop(0, tm, body, 0, unroll=8)

    def wait_batch(s):
        pltpu.make_async_copy(h_hbm.at[pl.ds(0, nrow)], xs_hbm.at[pl.ds(0, nrow)], sem_ref.at[s]).wait()

    nxt = lax.rem(i + 2, 3)

    @pl.when(i > 0)
    def _():
        wait_batch(nxt)

    @pl.when(i + 2 < nt)
    def _():
        load(i + 2, nxt).start()

    @pl.when(i == nt - 1)
    def _():
        wait_batch(slot)
        zero_ref[...] = jnp.zeros_like(zero_ref)

        def pad_copy(row):
            return pltpu.make_async_copy(zero_ref.at[pl.ds(0, 1)], xs_hbm.at[pl.ds(row, 1)], zsem_ref.at[0])

        def tail_copy(blk):
            row = pl.multiple_of(blk * EXPERT_BLOCK, EXPERT_BLOCK)
            return pltpu.make_async_copy(zero_ref, xs_hbm.at[pl.ds(row, EXPERT_BLOCK)], zsem_ref.at[1])

        def tail_start(blk, carry):
            tail_copy(blk).start()
            return carry
        lax.fori_loop(nu_ref[0], nb, tail_start, 0)

        def tail_wait(blk, carry):
            tail_copy(blk).wait()
            return carry
        lax.fori_loop(nu_ref[0], nb, tail_wait, 0)

        def per_expert(e, carry):
            def start(j, c2):
                pad_copy(zs_ref[e] + j).start()
                return c2
            lax.fori_loop(0, zn_ref[e], start, 0)

            def wait(j, c2):
                pad_copy(zs_ref[e] + j).wait()
                return c2
            lax.fori_loop(0, zn_ref[e], wait, 0)
            return carry
        lax.fori_loop(0, N_EXPERTS, per_expert, 0)


def _dispatch(h2, pos_t, zero_start, zero_n, n_used, nb, tm):
    t, d = h2.shape
    grid_spec = pltpu.PrefetchScalarGridSpec(
        num_scalar_prefetch=3,
        grid=(t // tm,),
        in_specs=[pl.BlockSpec((1, 1, TOP_K * tm), lambda i, zs, zn, nu: (i, 0, 0), memory_space=pltpu.SMEM),
                  pl.BlockSpec(memory_space=pl.ANY)],
        out_specs=pl.BlockSpec(memory_space=pl.ANY),
        scratch_shapes=[pltpu.VMEM((3, tm, d), F32), pltpu.VMEM((EXPERT_BLOCK, d), F32),
                        pltpu.SemaphoreType.DMA((3,)), pltpu.SemaphoreType.DMA((3,)),
                        pltpu.SemaphoreType.DMA((2,))],
    )
    return pl.pallas_call(
        functools.partial(_dispatch_body, tm, t // tm, nb),
        grid_spec=grid_spec,
        out_shape=jax.ShapeDtypeStruct((nb * EXPERT_BLOCK, d), F32),
        compiler_params=_cparams(("arbitrary",)),
        name="moe_dispatch",
    )(zero_start, zero_n, n_used, pos_t, h2)


def _experts_body(be_ref, nu_ref, x_ref, wg_ref, wu_ref, wd_ref, o_ref):
    used = pl.program_id(0) < nu_ref[0]

    @pl.when(used)
    def _():
        x = x_ref[...].astype(BF16)
        hg = jnp.dot(x, wg_ref[0], preferred_element_type=F32)
        hu = jnp.dot(x, wu_ref[0], preferred_element_type=F32)
        act = (_silu(hg) * hu).astype(BF16)
        o_ref[...] = jnp.dot(act, wd_ref[0], preferred_element_type=F32)

    @pl.when(jnp.logical_not(used))
    def _():
        o_ref[...] = jnp.zeros_like(o_ref)


def _experts(xs, block_e, n_used, wg, wu, wd):
    n_rows, d = xs.shape
    de = wg.shape[2]
    rb = EXPERT_BLOCK
    last = lambda i, nu: jnp.minimum(i, nu[0] - 1)
    grid_spec = pltpu.PrefetchScalarGridSpec(
        num_scalar_prefetch=2,
        grid=(n_rows // rb,),
        in_specs=[pl.BlockSpec((rb, d), lambda i, be, nu: (last(i, nu), 0)),
                  pl.BlockSpec((1, d, de), lambda i, be, nu: (be[last(i, nu)], 0, 0)),
                  pl.BlockSpec((1, d, de), lambda i, be, nu: (be[last(i, nu)], 0, 0)),
                  pl.BlockSpec((1, de, d), lambda i, be, nu: (be[last(i, nu)], 0, 0))],
        out_specs=pl.BlockSpec((rb, d), lambda i, be, nu: (i, 0)),
    )
    return pl.pallas_call(
        _experts_body,
        grid_spec=grid_spec,
        out_shape=jax.ShapeDtypeStruct((n_rows, d), F32),
        compiler_params=_cparams(("arbitrary",)),
        name="experts",
    )(block_e, n_used, xs, wg, wu, wd)


def _combine_body(tm, pos_ref, posn_ref, ys_hbm, w_ref, h_ref, x1_ref, g2_ref, fg_ref,
                  sg_ref, su_ref, sd_ref, o_ref, buf_ref, sem_ref):
    i = pl.program_id(0)
    nt = pl.num_programs(0)
    slot = lax.rem(i, 2)
    nrow = TOP_K * tm

    def row_copy(src, r, s):
        return pltpu.make_async_copy(ys_hbm.at[pl.ds(src, 1)], buf_ref.at[s, pl.ds(r, 1)], sem_ref.at[s])

    def issue(idx_ref, s):
        def body(r, carry):
            for k in range(TOP_K):
                row_copy(idx_ref[0, 0, k * tm + r], k * tm + r, s).start(priority=k % 2)
            return carry
        lax.fori_loop(0, tm, body, 0, unroll=4)

    @pl.when(i == 0)
    def _():
        issue(pos_ref, 0)

    @pl.when(i + 1 < nt)
    def _():
        issue(posn_ref, 1 - slot)

    pltpu.make_async_copy(ys_hbm.at[pl.ds(0, nrow)], buf_ref.at[slot], sem_ref.at[slot]).wait()

    w = w_ref[...]
    routed = jnp.zeros(h_ref.shape, F32)
    for k in range(TOP_K):
        routed = routed + w[:, k:k + 1] * buf_ref[slot, k * tm:(k + 1) * tm, :]
    xb = h_ref[...].astype(BF16)
    hg = jnp.dot(xb, sg_ref[...], preferred_element_type=F32)
    hu = jnp.dot(xb, su_ref[...], preferred_element_type=F32)
    shared = jnp.dot((_silu(hg) * hu).astype(BF16), sd_ref[...], preferred_element_type=F32)
    x2 = x1_ref[...] + g2_ref[0] * (routed + shared)
    r = lax.rsqrt(jnp.mean(x2 * x2, axis=-1, keepdims=True) + NORM_EPS)
    o_ref[0] = x2 * r * fg_ref[...]


def _tile_slots(pos, tm):
    k, t = pos.shape
    return pos.reshape(k, t // tm, tm).transpose(1, 0, 2).reshape(t // tm, 1, k * tm)


def _combine(ys, pos_t, w6, h2, x1, gate2, final_g, sg, su, sd, tokens_per_batch, tm):
    t, d = h2.shape
    nt = t // tm
    tpb = tokens_per_batch // tm
    ds = sg.shape[1]
    wpad = jnp.zeros((t, LANES), F32).at[:, :TOP_K].set(w6.T)
    tile = pl.BlockSpec((tm, d), lambda i: (i, 0))
    return pl.pallas_call(
        functools.partial(_combine_body, tm),
        grid=(nt,),
        in_specs=[pl.BlockSpec((1, 1, TOP_K * tm), lambda i: (i, 0, 0), memory_space=pltpu.SMEM),
                  pl.BlockSpec((1, 1, TOP_K * tm), lambda i: (jnp.minimum(i + 1, nt - 1), 0, 0),
                               memory_space=pltpu.SMEM),
                  pl.BlockSpec(memory_space=pl.ANY),
                  pl.BlockSpec((tm, LANES), lambda i: (i, 0)),
                  tile, tile,
                  pl.BlockSpec((1, 1, d), lambda i: (i // tpb, 0, 0)),
                  pl.BlockSpec((1, d), lambda i: (0, 0)),
                  pl.BlockSpec((d, ds), lambda i: (0, 0)),
                  pl.BlockSpec((d, ds), lambda i: (0, 0)),
                  pl.BlockSpec((ds, d), lambda i: (0, 0))],
        out_specs=pl.BlockSpec((1, tm, d), lambda i: (i // tpb, i % tpb, 0)),
        out_shape=jax.ShapeDtypeStruct((t // tokens_per_batch, tokens_per_batch, d), F32),
        scratch_shapes=[pltpu.VMEM((2, TOP_K * tm, d), F32), pltpu.SemaphoreType.DMA((2,))],
        compiler_params=_cparams(("arbitrary",)),
        name="moe_combine",
    )(pos_t, pos_t, ys, wpad, h2, x1, gate2, final_g.reshape(1, d), sg, su, sd)


def _ssd_params(ssm_conv_w, ssm_conv_b, ssm_dt_bias, ssm_a_log, ssm_d, ssm_norm_g):
    dx = SSM_HEADS * SSM_HEAD_DIM
    q = SSD_CHUNK
    pad = lambda v: jnp.zeros((2, 1, LANES), F32).at[:, 0, :SSM_HEADS].set(v)
    heads = jnp.arange(SSM_HEADS)
    e1 = (jnp.arange(dx)[None, :] // SSM_HEAD_DIM == heads[:, None]).astype(F32)
    e2 = (jnp.arange(SSM_HEADS * q)[None, :] // q == heads[:, None]).astype(F32)

    def stack(e):
        z = jnp.zeros((2 * LANES, e.shape[1]), F32)
        return z.at[:SSM_HEADS].set(e).at[LANES:LANES + SSM_HEADS].set(e).astype(BF16)

    return dict(
        cwx=ssm_conv_w[:, :dx], cbx=ssm_conv_b[None, :dx],
        cwbc=ssm_conv_w[:, dx:], cbbc=ssm_conv_b[None, dx:],
        dtb=pad(ssm_dt_bias), a=pad(-jnp.exp(ssm_a_log.astype(F32))),
        dexp=jnp.repeat(ssm_d, SSM_HEAD_DIM)[None, :], ng=ssm_norm_g[None, :],
        e=stack(e1), e2=stack(e2))


def kernel(x, c, ctx, c_ctx, ada_w, ada_b, norm1_g, norm2_g, w_in, ssm_conv_w, ssm_conv_b, ssm_dt_bias, ssm_a_log, ssm_d, ssm_norm_g, w_ssm_out, hy_conv_w, hy_conv_b, hy_w1, hy_b1, hy_w2, hy_b2, hy_w3, hy_b3, hy_freq, hy_w_out, hy_decay, hy_bias, w_hy_out, w_o, router_w, router_bias, e_w_gate, e_w_up, e_w_down, sh_w_gate, sh_w_up, sh_w_down, final_g):
    depth = ada_w.shape[0]
    assert depth == 1, "single-layer block"
    b, l, d = x.shape
    lc = ctx.shape[1]
    dx = SSM_HEADS * SSM_HEAD_DIM
    dbc = 2 * SSM_GROUPS * SSM_STATE
    assert dx == d and hy_bias.shape[-1] == d

    rows = -(-(b + 1) // BF16_SUBLANES) * BF16_SUBLANES
    cc = jnp.zeros((rows, d), F32).at[:b].set(c).at[b].set(c_ctx)
    mod_all = _matmul(_silu(cc).astype(BF16), ada_w[0].astype(BF16), F32, rows, 512, "adaln") + ada_b[0]
    mod = mod_all[:b].reshape(b, 6, d)
    mod_c = jnp.broadcast_to(mod_all[b].reshape(1, 6, d), (b, 6, d))

    w = w_in[0]
    o_dt, o_z, o_hy, o_g = dx + dbc, dx + dbc + 2 * SSM_HEADS, 2 * dx + dbc + 2 * SSM_HEADS, 2 * dx + dbc + 2 * SSM_HEADS + 3 * d
    w_main = jnp.concatenate([w[:, o_hy:o_g], w[:, o_g:], w[:, o_z:o_hy], w[:, :dx + dbc]], axis=1).astype(BF16)
    w_dt = (jnp.zeros((d, 2 * LANES), F32)
            .at[:, :SSM_HEADS].set(w[:, o_dt:o_dt + SSM_HEADS])
            .at[:, LANES:LANES + SSM_HEADS].set(w[:, o_dt + SSM_HEADS:o_z])).astype(BF16)
    col_ghy, col_gssd, col_z, col_xs, col_bc = 3, 4, 5, 6, 7 * (d // dbc)

    sp = _ssd_params(ssm_conv_w[0], ssm_conv_b[0], ssm_dt_bias[0], ssm_a_log[0], ssm_d[0], ssm_norm_g[0])

    hc = _normmod(ctx, norm1_g[0], mod_c[:, 0], mod_c[:, 1])
    w_ctx = w_main[:, col_xs * d:]
    proj_c = _matmul(hc, w_ctx, BF16, 1024, 512, "ctx_proj")
    dt_c = _matmul(hc, w_dt, F32, 1024, 2 * LANES, "ctx_dt")
    zero_state = jnp.zeros((b, 2, SSM_STATE, dx), F32)
    ctx_state = _ssd(proj_c, dt_c, zero_state, sp, False, 0, d // dbc, 0)

    hx = _normmod(x, norm1_g[0], mod[:, 0], mod[:, 1])
    proj = _matmul(hx, w_main, BF16, 1024, 512, "in_proj")
    dt_raw = _matmul(hx, w_dt, F32, 1024, 2 * LANES, "dt_proj")
    yn = _ssd(proj, dt_raw, ctx_state, sp, True, col_xs, col_bc, col_z)

    fb = min(l // 2, 256)
    ct = 512
    tables = _dft_tables(l, fb)
    hp = dict(hy_w1=hy_w1[0], hy_b1=hy_b1[0], hy_w2=hy_w2[0], hy_b2=hy_b2[0], hy_w3=hy_w3[0], hy_b3=hy_b3[0],
              hy_freq=hy_freq[0], hy_w_out=hy_w_out[0], hy_decay=hy_decay[0])
    coef, kmid = _filter_coefs(l, fb, ct, tables[0], tables[1], hp)
    zhy = _hyena(proj, b, hy_conv_w[0], hy_conv_b[0][None, :], tables, coef, kmid, hy_bias[0], ct, fb)

    x1, h2 = _outproj(zhy, yn, proj, x, mod[:, 2], mod[:, 3], mod[:, 4], norm2_g[0],
                      w_hy_out[0].astype(BF16), w_ssm_out[0].astype(BF16), w_o[0].astype(BF16),
                      col_ghy, col_gssd)

    e6, c6, w6, counts = _route(h2, router_w[0], router_bias[0])
    pos, block_e, n_used, zero_start, zero_n, nb = _dispatch_plan(e6, c6, counts, b * l * TOP_K)
    tm = min(l, 128)
    pos_t = _tile_slots(pos, tm)
    xs = _dispatch(h2, pos_t, zero_start, zero_n, n_used, nb, tm)
    ys = _experts(xs, block_e, n_used, e_w_gate[0].astype(BF16), e_w_up[0].astype(BF16),
                  e_w_down[0].astype(BF16))
    return _combine(ys, pos_t, w6, h2, x1, mod[:, 5].reshape(b, 1, d), final_g,
                    sh_w_gate[0].astype(BF16), sh_w_up[0].astype(BF16), sh_w_down[0].astype(BF16), l, tm)
```

```python
import functools
import math

import jax
import jax.numpy as jnp
from jax import lax
from jax.experimental import pallas as pl
from jax.experimental.pallas import tpu as pltpu

F32 = jnp.float32
BF16 = jnp.bfloat16

NORM_EPS = 1e-6
SSM_HEADS = 16
SSM_HEAD_DIM = 64
SSM_GROUPS = 2
SSM_STATE = 128
SSD_CHUNK = 128
HY_ORDER = 2
HY_BANDS = 16
N_EXPERTS = 64
TOP_K = 6
N_EXPERT_GROUPS = 8
TOPK_GROUPS = 4
ROUTED_SCALE = 2.5
EXPERT_BLOCK = 512

LANES = 128
BF16_SUBLANES = 16
VMEM_LIMIT = 56 * 1024 * 1024


def _cparams(sem):
    return pltpu.CompilerParams(dimension_semantics=sem, vmem_limit_bytes=VMEM_LIMIT)


def _silu(x):
    return x * jax.nn.sigmoid(x)


def _split_bf16(q):
    hi = q.astype(BF16)
    lo = (q - hi.astype(F32)).astype(BF16)
    return hi, lo


def _norm_proj_body(x_ref, g_ref, sh_ref, sc_ref, w_ref, wdt_ref, o_ref, dt_ref, hx_ref):
    @pl.when(pl.program_id(1) == 0)
    def _():
        x = x_ref[0]
        r = lax.rsqrt(jnp.mean(x * x, axis=-1, keepdims=True) + NORM_EPS)
        hx = ((x * r * g_ref[...]) * (1.0 + sc_ref[0]) + sh_ref[0]).astype(BF16)
        hx_ref[...] = hx
        dt_ref[...] = jnp.dot(hx, wdt_ref[...], preferred_element_type=F32)

    o_ref[...] = jnp.dot(hx_ref[...], w_ref[...], preferred_element_type=F32).astype(o_ref.dtype)


def _norm_proj(x, g, shift, scale, w, w_dt, tm, tn, name):
    b, l, d = x.shape
    n, ndt = w.shape[1], w_dt.shape[1]
    tm, tn = min(tm, l), min(tn, n)
    assert l % tm == 0 and n % tn == 0, (l, n, tm, tn)
    npb = l // tm
    row = pl.BlockSpec((1, 1, d), lambda i, j: (i // npb, 0, 0))
    return pl.pallas_call(
        _norm_proj_body,
        grid=(b * npb, n // tn),
        in_specs=[pl.BlockSpec((1, tm, d), lambda i, j: (i // npb, i % npb, 0)),
                  pl.BlockSpec((1, d), lambda i, j: (0, 0)), row, row,
                  pl.BlockSpec((d, tn), lambda i, j: (0, j)),
                  pl.BlockSpec((d, ndt), lambda i, j: (0, 0))],
        out_specs=[pl.BlockSpec((tm, tn), lambda i, j: (i, j)), pl.BlockSpec((tm, ndt), lambda i, j: (i, 0))],
        out_shape=[jax.ShapeDtypeStruct((b * l, n), BF16), jax.ShapeDtypeStruct((b * l, ndt), F32)],
        scratch_shapes=[pltpu.VMEM((tm, d), BF16)],
        compiler_params=_cparams(("parallel", "arbitrary")),
        name=name,
    )(x, g.reshape(1, d), shift.reshape(b, 1, d), scale.reshape(b, 1, d), w, w_dt)


def _mm_body(a_ref, w_ref, o_ref):
    o_ref[...] = jnp.dot(a_ref[...], w_ref[...], preferred_element_type=F32).astype(o_ref.dtype)


def _matmul(a, w, out_dtype, tm, tn, name):
    m, k = a.shape
    n = w.shape[1]
    tm, tn = min(tm, m), min(tn, n)
    assert m % tm == 0 and n % tn == 0, (m, n, tm, tn)
    return pl.pallas_call(
        _mm_body,
        grid=(m // tm, n // tn),
        in_specs=[pl.BlockSpec((tm, k), lambda i, j: (i, 0)),
                  pl.BlockSpec((k, tn), lambda i, j: (0, j))],
        out_specs=pl.BlockSpec((tm, tn), lambda i, j: (i, j)),
        out_shape=jax.ShapeDtypeStruct((m, n), out_dtype),
        compiler_params=_cparams(("parallel", "arbitrary")),
        name=name,
    )(a, w)


def _conv3_roll(u, w_ref, b_ref):
    q = u.shape[0]
    row = lax.broadcasted_iota(jnp.int32, u.shape, 0)
    um1 = jnp.where(row == 0, 0.0, pltpu.roll(u, 1, 0))
    up1 = jnp.where(row == q - 1, 0.0, pltpu.roll(u, q - 1, 0))
    return um1 * w_ref[0:1, :] + u * w_ref[1:2, :] + up1 * w_ref[2:3, :] + b_ref[...]


def _conv3(ext, off, rows, w_ref, b_ref, keep_prev=1, keep_next=1):
    n = ext.shape[0]
    r = lax.broadcasted_iota(jnp.int32, (rows, n), 0)
    c = lax.broadcasted_iota(jnp.int32, (rows, n), 1)
    down = ((c == r + (off - 1)) & (r + keep_prev > 0)).astype(BF16)
    up = ((c == r + (off + 1)) & (r - keep_next < rows - 1)).astype(BF16)
    um1 = jnp.dot(down, ext, preferred_element_type=F32)
    up1 = jnp.dot(up, ext, preferred_element_type=F32)
    u = ext[off:off + rows].astype(F32)
    return um1 * w_ref[0:1, :] + u * w_ref[1:2, :] + up1 * w_ref[2:3, :] + b_ref[...]


def _ssd_body(nc, emit_y, xs_ref, xsp_ref, xsn_ref, bc_ref, bcp_ref, bcn_ref, dt_ref, z_ref,
              cwx_ref, cbx_ref, cwbc_ref, cbbc_ref, dtb_ref, a_ref, dexp_ref, ng_ref,
              e_ref, e2_ref, init_ref, o_ref, st_ref, ysc_ref):
    q = SSD_CHUNK
    p = pl.program_id(1)
    c = pl.program_id(2)
    is_fwd = p == 1
    ci = jnp.where(is_fwd, c, nc - 1 - c)

    @pl.when(c == 0)
    def _():
        st_ref[...] = init_ref[0, 0]

    keep_prev = (ci > 0).astype(jnp.int32)
    keep_next = (ci < nc - 1).astype(jnp.int32)
    halo = BF16_SUBLANES
    xs = _silu(_conv3(jnp.concatenate([xsp_ref[...], xs_ref[...], xsn_ref[...]], axis=0), halo, q,
                      cwx_ref, cbx_ref, keep_prev, keep_next))
    bc = _silu(_conv3(jnp.concatenate([bcp_ref[...], bc_ref[...], bcn_ref[...]], axis=0), halo, q,
                      cwbc_ref, cbbc_ref, keep_prev, keep_next))
    ng2 = SSM_GROUPS * SSM_STATE
    bm, cm = bc[:, :ng2], bc[:, ng2:]

    dtr = dt_ref[...] + dtb_ref[0]
    dt = jnp.maximum(dtr, 0.0) + jnp.log(1.0 + jnp.exp(-jnp.abs(dtr)))
    a = dt * a_ref[0]

    rr = lax.broadcasted_iota(jnp.int32, (q, q), 0)
    cc = lax.broadcasted_iota(jnp.int32, (q, q), 1)
    mask = (rr - cc) * jnp.where(is_fwd, 1, -1) >= 0
    tri = mask.astype(BF16)
    a_hi, a_lo = _split_bf16(a)
    acs2 = jnp.dot(tri, jnp.concatenate([a_hi, a_lo], axis=1), preferred_element_type=F32)
    acs = acs2[:, :LANES] + acs2[:, LANES:]
    c_hi, c_lo = _split_bf16(acs)
    acs_cat = jnp.concatenate([c_hi, c_lo], axis=1)
    ex = jnp.dot(acs_cat, e_ref[...], preferred_element_type=F32)
    d_hi, d_lo = _split_bf16(dt)
    dt_exp = jnp.dot(jnp.concatenate([d_hi, d_lo], axis=1), e_ref[...], preferred_element_type=F32)
    ex_end = jnp.where(is_fwd, ex[q - 1:q, :], ex[0:1, :])

    xdt = xs * dt_exp
    xw = (xdt * jnp.exp(ex_end - ex)).astype(BF16)
    state = st_ref[...]

    if emit_y:
        acs_t = acs.T
        cb_all = jnp.dot(acs_cat, e2_ref[...], preferred_element_type=F32)
        lane = lax.broadcasted_iota(jnp.int32, (q, LANES), 1)
        state_bf = state.astype(BF16)
        hpg = SSM_HEADS // SSM_GROUPS
        gw = hpg * SSM_HEAD_DIM
        pieces = []
        for g in range(SSM_GROUPS):
            cg = cm[:, g * SSM_STATE:(g + 1) * SSM_STATE].astype(BF16)
            bg = bm[:, g * SSM_STATE:(g + 1) * SSM_STATE].astype(BF16)
            cbg = lax.dot_general(cg, bg, (((1,), (1,)), ((), ())), preferred_element_type=F32)
            yoff = jnp.dot(cg, state_bf[:, g * gw:(g + 1) * gw], preferred_element_type=F32)
            yoff = yoff * jnp.exp(ex[:, g * gw:(g + 1) * gw])
            for j in range(hpg // 2):
                h0 = g * hpg + 2 * j
                ms = []
                for h in (h0, h0 + 1):
                    seg = cb_all[:, h * q:(h + 1) * q] - acs_t[h:h + 1, :]
                    ms.append((cbg * jnp.exp(jnp.where(mask, seg, -jnp.inf))).astype(BF16))
                m2 = jnp.concatenate(ms, axis=1)
                xp = xdt[:, h0 * SSM_HEAD_DIM:(h0 + 2) * SSM_HEAD_DIM]
                x2 = jnp.concatenate([jnp.where(lane < SSM_HEAD_DIM, xp, 0.0),
                                      jnp.where(lane >= SSM_HEAD_DIM, xp, 0.0)], axis=0).astype(BF16)
                yd = jnp.dot(m2, x2, preferred_element_type=F32)
                pieces.append(yd + yoff[:, 2 * j * SSM_HEAD_DIM:(2 * j + 2) * SSM_HEAD_DIM])
        y = jnp.concatenate(pieces, axis=1)
        row0 = pl.multiple_of(ci * q, q)

        @pl.when(p == 0)
        def _():
            ysc_ref[pl.ds(row0, q), :] = y

        @pl.when(p == 1)
        def _():
            yt = y + ysc_ref[pl.ds(row0, q), :] + xs * dexp_ref[...]
            v = yt * _silu(z_ref[...].astype(F32))
            outs = []
            for g in range(SSM_GROUPS):
                vg = v[:, g * gw:(g + 1) * gw]
                outs.append(vg * lax.rsqrt(jnp.mean(vg * vg, axis=-1, keepdims=True) + NORM_EPS))
            o_ref[...] = (jnp.concatenate(outs, axis=1) * ng_ref[...]).astype(o_ref.dtype)

    new_parts = []
    hpg = SSM_HEADS // SSM_GROUPS
    gw = hpg * SSM_HEAD_DIM
    for g in range(SSM_GROUPS):
        bt = bm[:, g * SSM_STATE:(g + 1) * SSM_STATE].T.astype(BF16)
        new_parts.append(jnp.dot(bt, xw[:, g * gw:(g + 1) * gw], preferred_element_type=F32))
    st_new = state * jnp.exp(ex_end) + jnp.concatenate(new_parts, axis=1)
    st_ref[...] = st_new
    if not emit_y:
        @pl.when(c == nc - 1)
        def _():
            o_ref[0, 0] = st_new


def _ssd(proj, dt_raw, init, prm, emit_y, xs_col, bc_col, z_col):
    b = init.shape[0]
    l = proj.shape[0] // b
    q = SSD_CHUNK
    nc = l // q
    hb = q // BF16_SUBLANES
    nh = l // BF16_SUBLANES
    dx = SSM_HEADS * SSM_HEAD_DIM
    dbc = 2 * SSM_GROUPS * SSM_STATE

    def ci_of(p, c):
        return p * c + (1 - p) * (nc - 1 - c)

    def main(col):
        return lambda i, p, c: (i * nc + ci_of(p, c), col)

    def prev(col):
        return lambda i, p, c: (i * nh + jnp.maximum(ci_of(p, c) * hb - 1, 0), col)

    def nxt(col):
        return lambda i, p, c: (i * nh + jnp.minimum(ci_of(p, c) * hb + hb, nh - 1), col)

    const2 = lambda i, p, c: (0, 0)
    bydir = lambda i, p, c: (1 - p, 0, 0)
    in_specs = [
        pl.BlockSpec((q, dx), main(xs_col)),
        pl.BlockSpec((BF16_SUBLANES, dx), prev(xs_col)),
        pl.BlockSpec((BF16_SUBLANES, dx), nxt(xs_col)),
        pl.BlockSpec((q, dbc), main(bc_col)),
        pl.BlockSpec((BF16_SUBLANES, dbc), prev(bc_col)),
        pl.BlockSpec((BF16_SUBLANES, dbc), nxt(bc_col)),
        pl.BlockSpec((q, LANES), lambda i, p, c: (i * nc + ci_of(p, c), 1 - p)),
        pl.BlockSpec((q, dx), (lambda i, p, c: (i * nc + c * p, z_col))),
        pl.BlockSpec((3, dx), const2), pl.BlockSpec((1, dx), const2),
        pl.BlockSpec((3, dbc), const2), pl.BlockSpec((1, dbc), const2),
        pl.BlockSpec((1, 1, LANES), bydir), pl.BlockSpec((1, 1, LANES), bydir),
        pl.BlockSpec((1, dx), const2), pl.BlockSpec((1, dx), const2),
        pl.BlockSpec((2 * LANES, dx), const2),
        pl.BlockSpec((2 * LANES, SSM_HEADS * q), const2),
        pl.BlockSpec((1, 1, SSM_STATE, dx), lambda i, p, c: (i, 1 - p, 0, 0)),
    ]
    if emit_y:
        out_spec = pl.BlockSpec((q, dx), lambda i, p, c: (i * nc + c * p, 0))
        out_shape = jax.ShapeDtypeStruct((b * l, dx), BF16)
    else:
        out_spec = pl.BlockSpec((1, 1, SSM_STATE, dx), lambda i, p, c: (i, 1 - p, 0, 0))
        out_shape = jax.ShapeDtypeStruct((b, 2, SSM_STATE, dx), F32)
    return pl.pallas_call(
        functools.partial(_ssd_body, nc, emit_y),
        grid=(b, 2, nc),
        in_specs=in_specs,
        out_specs=out_spec,
        out_shape=out_shape,
        scratch_shapes=[pltpu.VMEM((SSM_STATE, dx), F32),
                        pltpu.VMEM((l if emit_y else q, dx), F32)],
        compiler_params=_cparams(("parallel", "arbitrary", "arbitrary")),
        name="ssd_scan" if emit_y else "ssd_ctx_state",
    )(proj, proj, proj, proj, proj, proj, dt_raw, proj,
      prm["cwx"], prm["cbx"], prm["cwbc"], prm["cbbc"], prm["dtb"], prm["a"], prm["dexp"], prm["ng"],
      prm["e"], prm["e2"], init)


def _hyena_body(nfb, fb, x1_ref, x2_ref, v_ref, cw1_ref, cw2_ref, cw3_ref, cb1_ref, cb2_ref, cb3_ref,
                we_ref, wo_ref, wie_ref, wio_ref, coef_ref, skip_ref, kmid_ref, o_ref,
                zf_ref, ze_ref, zo_ref, acce_ref, acco_ref, lc_ref, x1c_ref, x2c_ref, mid_ref):
    o = pl.program_id(2)
    k = pl.program_id(3)
    ns, l, _ = zf_ref.shape
    ct = ns * LANES
    h = l // 2
    slab = lambda j: slice(j * LANES, (j + 1) * LANES)

    def split_z():
        kre = jnp.where(o == 0, kmid_ref[0:1, :], kmid_ref[2:3, :])
        kim = jnp.where(o == 0, kmid_ref[1:2, :], kmid_ref[3:4, :])
        j4 = lax.broadcasted_iota(jnp.int32, (8, LANES), 0) & 3
        for j in range(ns):
            ze_ref[:, slab(j)] = zf_ref[j, pl.ds(0, h, stride=2), :].astype(BF16)
            zo_ref[:, slab(j)] = zf_ref[j, pl.ds(1, h, stride=2), :].astype(BF16)
            z8 = jnp.sum(zf_ref[j].reshape(l // 8, 8, LANES), axis=0)
            ure = z8[0:1] + z8[4:5] - z8[2:3] - z8[6:7]
            uim = z8[3:4] + z8[7:8] - z8[1:2] - z8[5:6]
            yre = ure * kre[:, slab(j)] - uim * kim[:, slab(j)]
            yim = ure * kim[:, slab(j)] + uim * kre[:, slab(j)]
            pat = jnp.where(j4 == 0, yre, jnp.where(j4 == 1, -yim, jnp.where(j4 == 2, -yre, yim)))
            mid_ref[:, slab(j)] = pat * (1.0 / l)

    @pl.when((o == 0) & (k == 0))
    def _():
        rows = min(l, 256)
        halo = BF16_SUBLANES
        for r0 in range(0, l, rows):
            lo, hi = max(r0 - halo, 0), min(r0 + rows + halo, l)
            mid = slice(r0 - lo, r0 - lo + rows)
            x1c_ref[r0:r0 + rows, :] = _conv3_roll(x1_ref[lo:hi, :].astype(F32), cw1_ref, cb1_ref)[mid].astype(BF16)
            x2c_ref[r0:r0 + rows, :] = _conv3_roll(x2_ref[lo:hi, :].astype(F32), cw2_ref, cb2_ref)[mid].astype(BF16)
            vc = _conv3_roll(v_ref[lo:hi, :].astype(F32), cw3_ref, cb3_ref)[mid]
            for j in range(ns):
                zf_ref[j, r0:r0 + rows, :] = vc[:, slab(j)]
        split_z()

    @pl.when(k == 0)
    def _():
        acce_ref[...] = jnp.zeros_like(acce_ref)
        acco_ref[...] = jnp.zeros_like(acco_ref)

    a = jnp.dot(we_ref[...], ze_ref[...], preferred_element_type=F32)
    b = jnp.dot(wo_ref[...], zo_ref[...], preferred_element_type=F32)
    lre, lim = a[:fb] + b[:fb], a[fb:] + b[fb:]
    hre, him = a[:fb] - b[:fb], b[fb:] - a[fb:]
    c = coef_ref[0, 0, 0]
    ylre, ylim = lre * c[0] - lim * c[1], lre * c[1] + lim * c[0]
    yhre, yhim = hre * c[2] - him * c[3], hre * c[3] + him * c[2]
    ge = jnp.concatenate([ylre + yhre, ylim - yhim], axis=0).astype(BF16)
    go = jnp.concatenate([ylre - yhre, ylim + yhim], axis=0).astype(BF16)
    acce_ref[...] += jnp.dot(wie_ref[0], ge, preferred_element_type=F32)
    acco_ref[...] += jnp.dot(wio_ref[0], go, preferred_element_type=F32)

    @pl.when(k == nfb - 1)
    def _():
        skip = jnp.where(o == 0, skip_ref[0:1, :], skip_ref[1:2, :])

        def long_conv(j):
            lc_ref[j, pl.ds(0, h, stride=2), :] = acce_ref[:, slab(j)]
            lc_ref[j, pl.ds(1, h, stride=2), :] = acco_ref[:, slab(j)]
            lc = lc_ref[j] + zf_ref[j] * skip[:, slab(j)]
            return (lc.reshape(l // 8, 8, LANES) + mid_ref[:, slab(j)][None]).reshape(l, LANES)

        @pl.when(o == 0)
        def _():
            for j in range(ns):
                zf_ref[j] = x1c_ref[:, slab(j)].astype(F32) * long_conv(j)
            split_z()

        @pl.when(o == 1)
        def _():
            for j in range(ns):
                o_ref[:, slab(j)] = (x2c_ref[:, slab(j)].astype(F32) * long_conv(j)).astype(o_ref.dtype)


def _hyena(proj, b, cw, cb, tables, coef, kmid, skip, ct, fb):
    l = proj.shape[0] // b
    d = skip.shape[1]
    h = l // 2
    nct = d // ct
    nfb = h // fb
    we, wo, wie, wio = tables

    def col(part):
        return lambda j, i, o, k: (i, part * nct + j)

    def wcol(part):
        return lambda j, i, o, k: (0, part * nct + j)

    return pl.pallas_call(
        functools.partial(_hyena_body, nfb, fb),
        grid=(nct, b, HY_ORDER, nfb),
        in_specs=[pl.BlockSpec((l, ct), col(0)), pl.BlockSpec((l, ct), col(1)), pl.BlockSpec((l, ct), col(2)),
                  pl.BlockSpec((3, ct), wcol(0)), pl.BlockSpec((3, ct), wcol(1)), pl.BlockSpec((3, ct), wcol(2)),
                  pl.BlockSpec((1, ct), wcol(0)), pl.BlockSpec((1, ct), wcol(1)), pl.BlockSpec((1, ct), wcol(2)),
                  pl.BlockSpec((2 * fb, h), lambda j, i, o, k: (k, 0)),
                  pl.BlockSpec((2 * fb, h), lambda j, i, o, k: (k, 0)),
                  pl.BlockSpec((1, h, 2 * fb), lambda j, i, o, k: (k, 0, 0)),
                  pl.BlockSpec((1, h, 2 * fb), lambda j, i, o, k: (k, 0, 0)),
                  pl.BlockSpec((1, 1, 1, 4, fb, ct), lambda j, i, o, k: (o, j, k, 0, 0, 0)),
                  pl.BlockSpec((HY_ORDER, ct), lambda j, i, o, k: (0, j)),
                  pl.BlockSpec((2 * HY_ORDER, ct), lambda j, i, o, k: (0, j))],
        out_specs=pl.BlockSpec((l, ct), lambda j, i, o, k: (i, j)),
        out_shape=jax.ShapeDtypeStruct((b * l, d), BF16),
        scratch_shapes=[pltpu.VMEM((ct // LANES, l, LANES), F32), pltpu.VMEM((h, ct), BF16),
                        pltpu.VMEM((h, ct), BF16), pltpu.VMEM((h, ct), F32), pltpu.VMEM((h, ct), F32),
                        pltpu.VMEM((ct // LANES, l, LANES), F32),
                        pltpu.VMEM((l, ct), BF16), pltpu.VMEM((l, ct), BF16), pltpu.VMEM((8, ct), F32)],
        compiler_params=_cparams(("parallel", "parallel", "arbitrary", "arbitrary")),
        name="hyena",
    )(proj, proj, proj, cw, cw, cw, cb, cb, cb, we, wo, wie, wio, coef, skip, kmid)


def _cis_table(rows, cols, n):
    t = jnp.arange(cols, dtype=jnp.int32)
    digit = 16

    def cis(mult, count):
        j = jnp.arange(count, dtype=jnp.int32)
        ang = (2.0 * math.pi / n) * ((j[:, None] * mult * t[None, :]) % n).astype(F32)
        return jnp.cos(ang), jnp.sin(ang)

    def cmul(a, b):
        (ca, sa), (cb_, sb) = a, b
        c = ca[:, None] * cb_[None] - sa[:, None] * sb[None]
        s = sa[:, None] * cb_[None] + ca[:, None] * sb[None]
        return c.reshape(-1, cols), s.reshape(-1, cols)

    c, s = cmul(cis(digit * digit, -(-rows // (digit * digit))), cmul(cis(digit, digit), cis(1, digit)))
    return c[:rows], s[:rows]


def _dft_tables(l, fb):
    n, h = 2 * l, l // 2
    f = jnp.arange(h, dtype=F32)[:, None]
    ce, se = _cis_table(h, h, l)
    ct_, st_ = jnp.cos((2.0 * math.pi / n) * f), jnp.sin((2.0 * math.pi / n) * f)
    co, so = ce * ct_ - se * st_, se * ct_ + ce * st_
    cf = jnp.where(f == 0, 1.0 / n, 2.0 / n)

    def blocked(re, im):
        return jnp.concatenate([re.reshape(h // fb, 1, fb, h), im.reshape(h // fb, 1, fb, h)], axis=1).reshape(l, h)

    def inverse(c, s):
        return blocked(cf * c, -(2.0 / n) * s).reshape(h // fb, 2 * fb, h).transpose(0, 2, 1).astype(BF16)

    return (blocked(ce, -se).astype(BF16), blocked(co, -so).astype(BF16), inverse(ce, se), inverse(co, so))


def _implicit_filters(l, hp, parity):
    hi = lax.Precision.HIGHEST
    pos = jnp.arange(parity, l, 2, dtype=F32)
    t = pos / max(l - 1, 1)
    bands = jnp.linspace(1e-4, HY_BANDS - 1, HY_BANDS, dtype=F32)
    ang = (2.0 * math.pi / l) * pos[:, None] * bands[None, :]
    z = jnp.concatenate([t[:, None], jnp.cos(ang), -jnp.sin(ang)], axis=-1)
    freq = hp["hy_freq"]
    hdn = jnp.sin(freq[0] * (jnp.dot(z, hp["hy_w1"], precision=hi) + hp["hy_b1"]))
    hdn = jnp.sin(freq[1] * (jnp.dot(hdn, hp["hy_w2"], precision=hi) + hp["hy_b2"]))
    hdn = jnp.sin(freq[2] * (jnp.dot(hdn, hp["hy_w3"], precision=hi) + hp["hy_b3"]))
    d = hp["hy_decay"].shape[-1]
    h = jnp.dot(hdn, hp["hy_w_out"], precision=hi).reshape(pos.shape[0], 2, HY_ORDER, d)
    window = jnp.exp(-t[:, None, None] * jnp.abs(hp["hy_decay"]))
    return h * window[:, None]


def _filter_coefs(l, fb, ct, we, wo, hp):
    h = l // 2
    d = hp["hy_decay"].shape[-1]
    nct, nfb = d // ct, h // fb
    he = _implicit_filters(l, hp, 0).at[0, 1].set(0.0)
    ho = _implicit_filters(l, hp, 1)
    he2 = he.reshape(h, 2 * HY_ORDER * d)
    ho2 = ho.reshape(h, 2 * HY_ORDER * d)

    def body(we_ref, wo_ref, fe_ref, fo_ref, be_ref, bo_ref, o_ref):
        def spectrum(e_ref, o_ref_):
            a = jnp.dot(we_ref[...], e_ref[...], preferred_element_type=F32)
            b = jnp.dot(wo_ref[...], o_ref_[...], preferred_element_type=F32)
            return a[:fb] + b[:fb], a[fb:] + b[fb:], a[:fb] - b[:fb], b[fb:] - a[fb:]
        f_lre, f_lim, f_hre, f_him = spectrum(fe_ref, fo_ref)
        b_lre, b_lim, b_hre, b_him = spectrum(be_ref, bo_ref)
        o_ref[0, 0, 0, 0] = f_lre + b_lre
        o_ref[0, 0, 0, 1] = f_lim - b_lim
        o_ref[0, 0, 0, 2] = f_hre + b_hre
        o_ref[0, 0, 0, 3] = f_him - b_him

    def col(direction):
        return lambda o, j, k: (0, (direction * HY_ORDER + o) * nct + j)

    wspec = pl.BlockSpec((2 * fb, h), lambda o, j, k: (k, 0))
    coef = pl.pallas_call(
        body,
        grid=(HY_ORDER, nct, nfb),
        in_specs=[wspec, wspec, pl.BlockSpec((h, ct), col(0)), pl.BlockSpec((h, ct), col(0)),
                  pl.BlockSpec((h, ct), col(1)), pl.BlockSpec((h, ct), col(1))],
        out_specs=pl.BlockSpec((1, 1, 1, 4, fb, ct), lambda o, j, k: (o, j, k, 0, 0, 0)),
        out_shape=jax.ShapeDtypeStruct((HY_ORDER, nct, nfb, 4, fb, ct), F32),
        compiler_params=_cparams(("parallel", "parallel", "arbitrary")),
        name="filter_spectrum",
    )(we, wo, he2.astype(BF16), ho2.astype(BF16), he2.astype(BF16), ho2.astype(BF16))

    alt = jnp.where(jnp.arange(h) % 2 == 0, 1.0, -1.0).astype(F32)
    hi = lax.Precision.HIGHEST
    fre = jnp.dot(alt, he2, precision=hi).reshape(2, HY_ORDER, d)
    fim = -jnp.dot(alt, ho2, precision=hi).reshape(2, HY_ORDER, d)
    kmid = jnp.stack([fre[0] + fre[1], fim[0] - fim[1]], axis=1).reshape(2 * HY_ORDER, d)
    return coef, kmid


def _outproj_body(zhy_ref, yn_ref, ghy_ref, gssd_ref, x_ref, g1_ref, sh2_ref, sc2_ref, n2g_ref,
                  why_ref, wssm_ref, wo_ref, x1_ref, h2_ref):
    hyo = jnp.dot(zhy_ref[...], why_ref[...], preferred_element_type=F32)
    sso = jnp.dot(yn_ref[...], wssm_ref[...], preferred_element_type=F32)
    merged = (jax.nn.sigmoid(ghy_ref[...].astype(F32)) * hyo
              + jax.nn.sigmoid(gssd_ref[...].astype(F32)) * sso)
    xmix = jnp.dot(merged.astype(BF16), wo_ref[...], preferred_element_type=F32)
    x1 = x_ref[0] + g1_ref[0] * xmix
    x1_ref[...] = x1
    r = lax.rsqrt(jnp.mean(x1 * x1, axis=-1, keepdims=True) + NORM_EPS)
    h2_ref[...] = (x1 * r * n2g_ref[...]) * (1.0 + sc2_ref[0]) + sh2_ref[0]


def _outproj(zhy, yn, proj, x, gate1, shift2, scale2, n2g, why, wssm, wo, ghy_col, gssd_col):
    b, l, d = x.shape
    tm = min(l, 512)
    nl = l // tm
    tile = lambda col: pl.BlockSpec((tm, d), lambda i, j: (i * nl + j, col))
    row = pl.BlockSpec((1, 1, d), lambda i, j: (i, 0, 0))
    wspec = pl.BlockSpec((d, d), lambda i, j: (0, 0))
    return pl.pallas_call(
        _outproj_body,
        grid=(b, nl),
        in_specs=[tile(0), tile(0), tile(ghy_col), tile(gssd_col),
                  pl.BlockSpec((1, tm, d), lambda i, j: (i, j, 0)), row, row, row,
                  pl.BlockSpec((1, d), lambda i, j: (0, 0)), wspec, wspec, wspec],
        out_specs=[tile(0), tile(0)],
        out_shape=[jax.ShapeDtypeStruct((b * l, d), F32), jax.ShapeDtypeStruct((b * l, d), F32)],
        compiler_params=_cparams(("parallel", "parallel")),
        name="merge_outproj",
    )(zhy, yn, proj, proj, x, gate1.reshape(b, 1, d), shift2.reshape(b, 1, d), scale2.reshape(b, 1, d),
      n2g.reshape(1, d), why, wssm, wo)


PER_GROUP = N_EXPERTS // N_EXPERT_GROUPS


def _route_body(tm, h_ref, wh_ref, wl_ref, bias_ref, e_ref, c_ref, w_ref, cnt_ref, carry_ref):
    ng, pg = N_EXPERT_GROUPS, PER_GROUP

    @pl.when(pl.program_id(0) == 0)
    def _():
        carry_ref[...] = jnp.zeros_like(carry_ref)

    x_hi, x_lo = _split_bf16(h_ref[...])
    nt = (((1,), (1,)), ((), ()))
    lg = (lax.dot_general(wh_ref[...], x_hi, nt, preferred_element_type=F32)
          + lax.dot_general(wh_ref[...], x_lo, nt, preferred_element_type=F32)
          + lax.dot_general(wl_ref[...], x_hi, nt, preferred_element_type=F32))
    sc = jax.nn.sigmoid(lg)
    sel = sc + bias_ref[...]

    mem = [sel[m * ng:(m + 1) * ng, :] for m in range(pg)]
    m1 = functools.reduce(jnp.maximum, mem)
    first = functools.reduce(jnp.minimum, [jnp.where(mem[m] == m1, m, pg) for m in range(pg)])
    m2 = functools.reduce(jnp.maximum, [jnp.where(first == m, -jnp.inf, mem[m]) for m in range(pg)])
    gs = m1 + m2
    gid = lax.broadcasted_iota(jnp.int32, gs.shape, 0)
    beats = jnp.zeros(gs.shape, jnp.int32)
    for g2 in range(ng):
        v = gs[g2:g2 + 1, :]
        beats = beats + ((v > gs) | ((v == gs) & (g2 < gid))).astype(jnp.int32)
    keep = beats < TOPK_GROUPS
    selm = jnp.concatenate([jnp.where(keep, mem[m], -jnp.inf) for m in range(pg)], axis=0)

    r = lax.broadcasted_iota(jnp.int32, selm.shape, 0)
    eid = (r % ng) * pg + r // ng
    rank = jnp.zeros(selm.shape, jnp.int32)
    for r2 in range(N_EXPERTS):
        v = selm[r2:r2 + 1, :]
        e2 = (r2 % ng) * pg + r2 // ng
        rank = rank + ((v > selm) | ((v == selm) & (e2 < eid))).astype(jnp.int32)
    chosen = rank < TOP_K
    wsel = jnp.where(chosen, sc, 0.0)
    wn = wsel / jnp.sum(wsel, axis=0, keepdims=True) * ROUTED_SCALE

    cb = chosen.astype(BF16)
    t0 = lax.broadcasted_iota(jnp.int32, (tm, tm), 0)
    t1 = lax.broadcasted_iota(jnp.int32, (tm, tm), 1)
    before = jnp.dot(cb, (t0 < t1).astype(BF16), preferred_element_type=F32) + carry_ref[...]
    carry = carry_ref[...] + jnp.dot(cb, jnp.ones((tm, tm), BF16), preferred_element_type=F32)
    carry_ref[...] = carry
    cnt_ref[...] = carry

    k8 = lax.broadcasted_iota(jnp.int32, e_ref.shape, 0)
    eidf = eid.astype(F32)
    e_out = jnp.zeros(e_ref.shape, F32)
    c_out = jnp.zeros(e_ref.shape, F32)
    w_out = jnp.zeros(e_ref.shape, F32)
    for k in range(TOP_K):
        mk = rank == k
        e_out = jnp.where(k8 == k, jnp.sum(jnp.where(mk, eidf, 0.0), axis=0, keepdims=True), e_out)
        c_out = jnp.where(k8 == k, jnp.sum(jnp.where(mk, before, 0.0), axis=0, keepdims=True), c_out)
        w_out = jnp.where(k8 == k, jnp.sum(jnp.where(mk, wn, 0.0), axis=0, keepdims=True), w_out)
    e_ref[...] = e_out.astype(jnp.int32)
    c_ref[...] = c_out.astype(jnp.int32)
    w_ref[...] = w_out


def _route(h2, router_w, router_bias):
    t, d = h2.shape
    tm = min(t, 256)
    ng, pg = N_EXPERT_GROUPS, PER_GROUP
    wt = router_w.T.reshape(ng, pg, d).transpose(1, 0, 2).reshape(N_EXPERTS, d)
    w_hi, w_lo = _split_bf16(wt)
    bias = jnp.broadcast_to(router_bias.astype(F32).reshape(ng, pg).T.reshape(N_EXPERTS, 1), (N_EXPERTS, tm))
    slot = pl.BlockSpec((8, tm), lambda i: (0, i))
    const = lambda shape: pl.BlockSpec(shape, lambda i: (0, 0))
    e8, c8, w8, cnt = pl.pallas_call(
        functools.partial(_route_body, tm),
        grid=(t // tm,),
        in_specs=[pl.BlockSpec((tm, d), lambda i: (i, 0)), const((N_EXPERTS, d)), const((N_EXPERTS, d)),
                  const((N_EXPERTS, tm))],
        out_specs=[slot, slot, slot, const((N_EXPERTS, tm))],
        out_shape=[jax.ShapeDtypeStruct((8, t), jnp.int32), jax.ShapeDtypeStruct((8, t), jnp.int32),
                   jax.ShapeDtypeStruct((8, t), F32), jax.ShapeDtypeStruct((N_EXPERTS, tm), F32)],
        scratch_shapes=[pltpu.VMEM((N_EXPERTS, tm), F32)],
        compiler_params=_cparams(("arbitrary",)),
        name="moe_route",
    )(h2, w_hi, w_lo, bias)
    counts = cnt[:, 0].reshape(pg, ng).T.reshape(N_EXPERTS).astype(jnp.int32)
    return e8[:TOP_K], c8[:TOP_K], w8[:TOP_K], counts


def _dispatch_plan(e6, c6, counts, n_slots):
    padded = (counts + EXPERT_BLOCK - 1) // EXPERT_BLOCK * EXPERT_BLOCK
    pad_end = jnp.cumsum(padded)
    pad_start = pad_end - padded
    nb = -(-n_slots // EXPERT_BLOCK) + N_EXPERTS
    blk = jnp.arange(nb, dtype=jnp.int32) * EXPERT_BLOCK
    block_e = jnp.minimum(jnp.sum((pad_end[None, :] <= blk[:, None]).astype(jnp.int32), axis=1), N_EXPERTS - 1)
    n_used = (pad_end[-1] // EXPERT_BLOCK).astype(jnp.int32).reshape(1)
    onehot = e6[..., None] == jnp.arange(N_EXPERTS, dtype=jnp.int32)
    pos = jnp.sum(jnp.where(onehot, pad_start, 0), axis=-1) + c6
    return pos.astype(jnp.int32), block_e.astype(jnp.int32), n_used, (pad_start + counts).astype(jnp.int32), \
        (padded - counts).astype(jnp.int32), nb


def _dispatch_body(tm, nt, nb, zs_ref, zn_ref, nu_ref, pos_ref, h_hbm, xs_hbm, buf_ref, zero_ref,
                   lsem_ref, sem_ref, zsem_ref):
    i = pl.program_id(0)
    slot = lax.rem(i, 3)
    nrow = TOP_K * tm

    def load(j, s):
        return pltpu.make_async_copy(h_hbm.at[pl.ds(pl.multiple_of(j * tm, tm), tm)], buf_ref.at[s], lsem_ref.at[s])

    @pl.when(i == 0)
    def _():
        load(0, 0).start()
        if nt > 1:
            load(1, 1).start()

    load(i, slot).wait()

    for s in range(3):
        @pl.when(slot == s)
        def _():
            def body(r, carry):
                for k in range(TOP_K):
                    pltpu.make_async_copy(buf_ref.at[s, pl.ds(r, 1)],
                                          xs_hbm.at[pl.ds(pos_ref[0, 0, k * tm + r], 1)],
                                          sem_ref.at[s]).start(priority=k % 2)
                return carry
            lax.fori_loop(0, tm, body, 0, unroll=8)

    def wait_batch(s):
        pltpu.make_async_copy(h_hbm.at[pl.ds(0, nrow)], xs_hbm.at[pl.ds(0, nrow)], sem_ref.at[s]).wait()

    nxt = lax.rem(i + 2, 3)

    @pl.when(i > 0)
    def _():
        wait_batch(nxt)

    @pl.when(i + 2 < nt)
    def _():
        load(i + 2, nxt).start()

    @pl.when(i == nt - 1)
    def _():
        wait_batch(slot)
        zero_ref[...] = jnp.zeros_like(zero_ref)

        def pad_copy(row):
            return pltpu.make_async_copy(zero_ref.at[pl.ds(0, 1)], xs_hbm.at[pl.ds(row, 1)], zsem_ref.at[0])

        def tail_copy(blk):
            row = pl.multiple_of(blk * EXPERT_BLOCK, EXPERT_BLOCK)
            return pltpu.make_async_copy(zero_ref, xs_hbm.at[pl.ds(row, EXPERT_BLOCK)], zsem_ref.at[1])

        def tail_start(blk, carry):
            tail_copy(blk).start()
            return carry
        lax.fori_loop(nu_ref[0], nb, tail_start, 0)

        def tail_wait(blk, carry):
            tail_copy(blk).wait()
            return carry
        lax.fori_loop(nu_ref[0], nb, tail_wait, 0)

        def per_expert(e, carry):
            def start(j, c2):
                pad_copy(zs_ref[e] + j).start()
                return c2
            lax.fori_loop(0, zn_ref[e], start, 0)

            def wait(j, c2):
                pad_copy(zs_ref[e] + j).wait()
                return c2
            lax.fori_loop(0, zn_ref[e], wait, 0)
            return carry
        lax.fori_loop(0, N_EXPERTS, per_expert, 0)


def _dispatch(h2, pos_t, zero_start, zero_n, n_used, nb, tm):
    t, d = h2.shape
    grid_spec = pltpu.PrefetchScalarGridSpec(
        num_scalar_prefetch=3,
        grid=(t // tm,),
        in_specs=[pl.BlockSpec((1, 1, TOP_K * tm), lambda i, zs, zn, nu: (i, 0, 0), memory_space=pltpu.SMEM),
                  pl.BlockSpec(memory_space=pl.ANY)],
        out_specs=pl.BlockSpec(memory_space=pl.ANY),
        scratch_shapes=[pltpu.VMEM((3, tm, d), F32), pltpu.VMEM((EXPERT_BLOCK, d), F32),
                        pltpu.SemaphoreType.DMA((3,)), pltpu.SemaphoreType.DMA((3,)),
                        pltpu.SemaphoreType.DMA((2,))],
    )
    return pl.pallas_call(
        functools.partial(_dispatch_body, tm, t // tm, nb),
        grid_spec=grid_spec,
        out_shape=jax.ShapeDtypeStruct((nb * EXPERT_BLOCK, d), F32),
        compiler_params=_cparams(("arbitrary",)),
        name="moe_dispatch",
    )(zero_start, zero_n, n_used, pos_t, h2)


def _experts_body(be_ref, nu_ref, x_ref, wg_ref, wu_ref, wd_ref, o_ref):
    used = pl.program_id(0) < nu_ref[0]

    @pl.when(used)
    def _():
        x = x_ref[...].astype(BF16)
        hg = jnp.dot(x, wg_ref[0], preferred_element_type=F32)
        hu = jnp.dot(x, wu_ref[0], preferred_element_type=F32)
        act = (_silu(hg) * hu).astype(BF16)
        o_ref[...] = jnp.dot(act, wd_ref[0], preferred_element_type=F32)

    @pl.when(jnp.logical_not(used))
    def _():
        o_ref[...] = jnp.zeros_like(o_ref)


def _experts(xs, block_e, n_used, wg, wu, wd):
    n_rows, d = xs.shape
    de = wg.shape[2]
    rb = EXPERT_BLOCK
    last = lambda i, nu: jnp.minimum(i, nu[0] - 1)
    grid_spec = pltpu.PrefetchScalarGridSpec(
        num_scalar_prefetch=2,
        grid=(n_rows // rb,),
        in_specs=[pl.BlockSpec((rb, d), lambda i, be, nu: (last(i, nu), 0)),
                  pl.BlockSpec((1, d, de), lambda i, be, nu: (be[last(i, nu)], 0, 0)),
                  pl.BlockSpec((1, d, de), lambda i, be, nu: (be[last(i, nu)], 0, 0)),
                  pl.BlockSpec((1, de, d), lambda i, be, nu: (be[last(i, nu)], 0, 0))],
        out_specs=pl.BlockSpec((rb, d), lambda i, be, nu: (i, 0)),
    )
    return pl.pallas_call(
        _experts_body,
        grid_spec=grid_spec,
        out_shape=jax.ShapeDtypeStruct((n_rows, d), F32),
        compiler_params=_cparams(("arbitrary",)),
        name="experts",
    )(block_e, n_used, xs, wg, wu, wd)


def _combine_body(tm, pos_ref, posn_ref, ys_hbm, w_ref, h_ref, x1_ref, g2_ref, fg_ref,
                  sg_ref, su_ref, sd_ref, o_ref, buf_ref, sem_ref):
    i = pl.program_id(0)
    nt = pl.num_programs(0)
    slot = lax.rem(i, 2)
    nrow = TOP_K * tm

    def row_copy(src, r, s):
        return pltpu.make_async_copy(ys_hbm.at[pl.ds(src, 1)], buf_ref.at[s, pl.ds(r, 1)], sem_ref.at[s])

    def issue(idx_ref, s):
        def body(r, carry):
            for k in range(TOP_K):
                row_copy(idx_ref[0, 0, k * tm + r], k * tm + r, s).start(priority=k % 2)
            return carry
        lax.fori_loop(0, tm, body, 0, unroll=4)

    @pl.when(i == 0)
    def _():
        issue(pos_ref, 0)

    for s in (0, 1):
        @pl.when((i + 1 < nt) & (slot == 1 - s))
        def _():
            issue(posn_ref, s)

    pltpu.make_async_copy(ys_hbm.at[pl.ds(0, nrow)], buf_ref.at[slot], sem_ref.at[slot]).wait()

    w = w_ref[...]
    routed = jnp.zeros(h_ref.shape, F32)
    for k in range(TOP_K):
        routed = routed + w[:, k:k + 1] * buf_ref[slot, k * tm:(k + 1) * tm, :]
    xb = h_ref[...].astype(BF16)
    hg = jnp.dot(xb, sg_ref[...], preferred_element_type=F32)
    hu = jnp.dot(xb, su_ref[...], preferred_element_type=F32)
    shared = jnp.dot((_silu(hg) * hu).astype(BF16), sd_ref[...], preferred_element_type=F32)
    x2 = x1_ref[...] + g2_ref[0] * (routed + shared)
    r = lax.rsqrt(jnp.mean(x2 * x2, axis=-1, keepdims=True) + NORM_EPS)
    o_ref[0] = x2 * r * fg_ref[...]


def _tile_slots(pos, tm):
    k, t = pos.shape
    return pos.reshape(k, t // tm, tm).transpose(1, 0, 2).reshape(t // tm, 1, k * tm)


def _combine(ys, pos_t, w6, h2, x1, gate2, final_g, sg, su, sd, tokens_per_batch, tm):
    t, d = h2.shape
    nt = t // tm
    tpb = tokens_per_batch // tm
    ds = sg.shape[1]
    wpad = jnp.zeros((t, LANES), F32).at[:, :TOP_K].set(w6.T)
    tile = pl.BlockSpec((tm, d), lambda i: (i, 0))
    return pl.pallas_call(
        functools.partial(_combine_body, tm),
        grid=(nt,),
        in_specs=[pl.BlockSpec((1, 1, TOP_K * tm), lambda i: (i, 0, 0), memory_space=pltpu.SMEM),
                  pl.BlockSpec((1, 1, TOP_K * tm), lambda i: (jnp.minimum(i + 1, nt - 1), 0, 0),
                               memory_space=pltpu.SMEM),
                  pl.BlockSpec(memory_space=pl.ANY),
                  pl.BlockSpec((tm, LANES), lambda i: (i, 0)),
                  tile, tile,
                  pl.BlockSpec((1, 1, d), lambda i: (i // tpb, 0, 0)),
                  pl.BlockSpec((1, d), lambda i: (0, 0)),
                  pl.BlockSpec((d, ds), lambda i: (0, 0)),
                  pl.BlockSpec((d, ds), lambda i: (0, 0)),
                  pl.BlockSpec((ds, d), lambda i: (0, 0))],
        out_specs=pl.BlockSpec((1, tm, d), lambda i: (i // tpb, i % tpb, 0)),
        out_shape=jax.ShapeDtypeStruct((t // tokens_per_batch, tokens_per_batch, d), F32),
        scratch_shapes=[pltpu.VMEM((2, TOP_K * tm, d), F32), pltpu.SemaphoreType.DMA((2,))],
        compiler_params=_cparams(("arbitrary",)),
        name="moe_combine",
    )(pos_t, pos_t, ys, wpad, h2, x1, gate2, final_g.reshape(1, d), sg, su, sd)


def _ssd_params(ssm_conv_w, ssm_conv_b, ssm_dt_bias, ssm_a_log, ssm_d, ssm_norm_g):
    dx = SSM_HEADS * SSM_HEAD_DIM
    q = SSD_CHUNK
    pad = lambda v: jnp.zeros((2, 1, LANES), F32).at[:, 0, :SSM_HEADS].set(v)
    heads = jnp.arange(SSM_HEADS)
    e1 = (jnp.arange(dx)[None, :] // SSM_HEAD_DIM == heads[:, None]).astype(F32)
    e2 = (jnp.arange(SSM_HEADS * q)[None, :] // q == heads[:, None]).astype(F32)

    def stack(e):
        z = jnp.zeros((2 * LANES, e.shape[1]), F32)
        return z.at[:SSM_HEADS].set(e).at[LANES:LANES + SSM_HEADS].set(e).astype(BF16)

    return dict(
        cwx=ssm_conv_w[:, :dx], cbx=ssm_conv_b[None, :dx],
        cwbc=ssm_conv_w[:, dx:], cbbc=ssm_conv_b[None, dx:],
        dtb=pad(ssm_dt_bias), a=pad(-jnp.exp(ssm_a_log.astype(F32))),
        dexp=jnp.repeat(ssm_d, SSM_HEAD_DIM)[None, :], ng=ssm_norm_g[None, :],
        e=stack(e1), e2=stack(e2))


def kernel(x, c, ctx, c_ctx, ada_w, ada_b, norm1_g, norm2_g, w_in, ssm_conv_w, ssm_conv_b, ssm_dt_bias, ssm_a_log, ssm_d, ssm_norm_g, w_ssm_out, hy_conv_w, hy_conv_b, hy_w1, hy_b1, hy_w2, hy_b2, hy_w3, hy_b3, hy_freq, hy_w_out, hy_decay, hy_bias, w_hy_out, w_o, router_w, router_bias, e_w_gate, e_w_up, e_w_down, sh_w_gate, sh_w_up, sh_w_down, final_g):
    depth = ada_w.shape[0]
    assert depth == 1, "single-layer block"
    b, l, d = x.shape
    lc = ctx.shape[1]
    dx = SSM_HEADS * SSM_HEAD_DIM
    dbc = 2 * SSM_GROUPS * SSM_STATE
    assert dx == d and hy_bias.shape[-1] == d

    rows = -(-(b + 1) // BF16_SUBLANES) * BF16_SUBLANES
    cc = jnp.zeros((rows, d), F32).at[:b].set(c).at[b].set(c_ctx)
    mod_all = _matmul(_silu(cc).astype(BF16), ada_w[0].astype(BF16), F32, rows, 512, "adaln") + ada_b[0]
    mod = mod_all[:b].reshape(b, 6, d)
    mod_c = jnp.broadcast_to(mod_all[b].reshape(1, 6, d), (b, 6, d))

    w = w_in[0]
    o_dt, o_z, o_hy, o_g = dx + dbc, dx + dbc + 2 * SSM_HEADS, 2 * dx + dbc + 2 * SSM_HEADS, 2 * dx + dbc + 2 * SSM_HEADS + 3 * d
    w_main = jnp.concatenate([w[:, o_hy:o_g], w[:, o_g:], w[:, o_z:o_hy], w[:, :dx + dbc]], axis=1).astype(BF16)
    w_dt = (jnp.zeros((d, 2 * LANES), F32)
            .at[:, :SSM_HEADS].set(w[:, o_dt:o_dt + SSM_HEADS])
            .at[:, LANES:LANES + SSM_HEADS].set(w[:, o_dt + SSM_HEADS:o_z])).astype(BF16)
    col_ghy, col_gssd, col_z, col_xs, col_bc = 3, 4, 5, 6, 7 * (d // dbc)

    sp = _ssd_params(ssm_conv_w[0], ssm_conv_b[0], ssm_dt_bias[0], ssm_a_log[0], ssm_d[0], ssm_norm_g[0])

    w_ctx = w_main[:, col_xs * d:]
    proj_c, dt_c = _norm_proj(ctx, norm1_g[0], mod_c[:, 0], mod_c[:, 1], w_ctx, w_dt, 1024, 512, "ctx_proj")
    zero_state = jnp.zeros((b, 2, SSM_STATE, dx), F32)
    ctx_state = _ssd(proj_c, dt_c, zero_state, sp, False, 0, d // dbc, 0)

    proj, dt_raw = _norm_proj(x, norm1_g[0], mod[:, 0], mod[:, 1], w_main, w_dt, 1024, 1536, "in_proj")
    yn = _ssd(proj, dt_raw, ctx_state, sp, True, col_xs, col_bc, col_z)

    fb = min(l // 2, 256)
    ct = 512
    tables = _dft_tables(l, fb)
    hp = dict(hy_w1=hy_w1[0], hy_b1=hy_b1[0], hy_w2=hy_w2[0], hy_b2=hy_b2[0], hy_w3=hy_w3[0], hy_b3=hy_b3[0],
              hy_freq=hy_freq[0], hy_w_out=hy_w_out[0], hy_decay=hy_decay[0])
    coef, kmid = _filter_coefs(l, fb, ct, tables[0], tables[1], hp)
    zhy = _hyena(proj, b, hy_conv_w[0], hy_conv_b[0][None, :], tables, coef, kmid, hy_bias[0], ct, fb)

    x1, h2 = _outproj(zhy, yn, proj, x, mod[:, 2], mod[:, 3], mod[:, 4], norm2_g[0],
                      w_hy_out[0].astype(BF16), w_ssm_out[0].astype(BF16), w_o[0].astype(BF16),
                      col_ghy, col_gssd)

    e6, c6, w6, counts = _route(h2, router_w[0], router_bias[0])
    pos, block_e, n_used, zero_start, zero_n, nb = _dispatch_plan(e6, c6, counts, b * l * TOP_K)
    tm = min(l, 128)
    pos_t = _tile_slots(pos, tm)
    xs = _dispatch(h2, pos_t, zero_start, zero_n, n_used, nb, tm)
    ys = _experts(xs, block_e, n_used, e_w_gate[0].astype(BF16), e_w_up[0].astype(BF16),
                  e_w_down[0].astype(BF16))
    return _combine(ys, pos_t, w6, h2, x1, mod[:, 5].reshape(b, 1, d), final_g,
                    sh_w_gate[0].astype(BF16), sh_w_up[0].astype(BF16), sh_w_down[0].astype(BF16), l, tm)
```

```python
import functools
import math

import jax
import jax.numpy as jnp
from jax import lax
from jax.experimental import pallas as pl
from jax.experimental.pallas import tpu as pltpu

F32 = jnp.float32
BF16 = jnp.bfloat16

NORM_EPS = 1e-6
SSM_HEADS = 16
SSM_HEAD_DIM = 64
SSM_GROUPS = 2
SSM_STATE = 128
SSD_CHUNK = 128
HY_ORDER = 2
HY_BANDS = 16
N_EXPERTS = 64
TOP_K = 6
N_EXPERT_GROUPS = 8
TOPK_GROUPS = 4
ROUTED_SCALE = 2.5
EXPERT_BLOCK = 512

LANES = 128
BF16_SUBLANES = 16
VMEM_LIMIT = 56 * 1024 * 1024


def _cparams(sem):
    return pltpu.CompilerParams(dimension_semantics=sem, vmem_limit_bytes=VMEM_LIMIT)


def _silu(x):
    return x * jax.nn.sigmoid(x)


def _pack_bf16_pairs(x):
    half = x.shape[1] // 2
    u = pltpu.bitcast(x.astype(BF16).astype(F32), jnp.uint32)
    return u[:, half:] | (u[:, :half] >> 16)


def _unpack_bf16_pairs(v):
    lo = pltpu.bitcast(v << 16, F32)
    hi = pltpu.bitcast(v & jnp.uint32(0xFFFF0000), F32)
    return lo, hi


def _split_bf16(q):
    hi = q.astype(BF16)
    lo = (q - hi.astype(F32)).astype(BF16)
    return hi, lo


def _norm_proj_body(x_ref, g_ref, sh_ref, sc_ref, w_ref, wdt_ref, o_ref, dt_ref, hx_ref):
    @pl.when(pl.program_id(1) == 0)
    def _():
        x = x_ref[0]
        r = lax.rsqrt(jnp.mean(x * x, axis=-1, keepdims=True) + NORM_EPS)
        hx = ((x * r * g_ref[...]) * (1.0 + sc_ref[0]) + sh_ref[0]).astype(BF16)
        hx_ref[...] = hx
        dt_ref[...] = jnp.dot(hx, wdt_ref[...], preferred_element_type=F32)

    o_ref[...] = jnp.dot(hx_ref[...], w_ref[...], preferred_element_type=F32).astype(o_ref.dtype)


def _norm_proj(x, g, shift, scale, w, w_dt, tm, tn, name):
    b, l, d = x.shape
    n, ndt = w.shape[1], w_dt.shape[1]
    tm, tn = min(tm, l), min(tn, n)
    assert l % tm == 0 and n % tn == 0, (l, n, tm, tn)
    npb = l // tm
    row = pl.BlockSpec((1, 1, d), lambda i, j: (i // npb, 0, 0))
    return pl.pallas_call(
        _norm_proj_body,
        grid=(b * npb, n // tn),
        in_specs=[pl.BlockSpec((1, tm, d), lambda i, j: (i // npb, i % npb, 0)),
                  pl.BlockSpec((1, d), lambda i, j: (0, 0)), row, row,
                  pl.BlockSpec((d, tn), lambda i, j: (0, j)),
                  pl.BlockSpec((d, ndt), lambda i, j: (0, 0))],
        out_specs=[pl.BlockSpec((tm, tn), lambda i, j: (i, j)), pl.BlockSpec((tm, ndt), lambda i, j: (i, 0))],
        out_shape=[jax.ShapeDtypeStruct((b * l, n), BF16), jax.ShapeDtypeStruct((b * l, ndt), F32)],
        scratch_shapes=[pltpu.VMEM((tm, d), BF16)],
        compiler_params=_cparams(("parallel", "arbitrary")),
        name=name,
    )(x, g.reshape(1, d), shift.reshape(b, 1, d), scale.reshape(b, 1, d), w, w_dt)


def _mm_body(a_ref, w_ref, o_ref):
    o_ref[...] = jnp.dot(a_ref[...], w_ref[...], preferred_element_type=F32).astype(o_ref.dtype)


def _matmul(a, w, out_dtype, tm, tn, name):
    m, k = a.shape
    n = w.shape[1]
    tm, tn = min(tm, m), min(tn, n)
    assert m % tm == 0 and n % tn == 0, (m, n, tm, tn)
    return pl.pallas_call(
        _mm_body,
        grid=(m // tm, n // tn),
        in_specs=[pl.BlockSpec((tm, k), lambda i, j: (i, 0)),
                  pl.BlockSpec((k, tn), lambda i, j: (0, j))],
        out_specs=pl.BlockSpec((tm, tn), lambda i, j: (i, j)),
        out_shape=jax.ShapeDtypeStruct((m, n), out_dtype),
        compiler_params=_cparams(("parallel", "arbitrary")),
        name=name,
    )(a, w)


def _conv3_roll(u, w_ref, b_ref):
    q = u.shape[0]
    row = lax.broadcasted_iota(jnp.int32, u.shape, 0)
    um1 = jnp.where(row == 0, 0.0, pltpu.roll(u, 1, 0))
    up1 = jnp.where(row == q - 1, 0.0, pltpu.roll(u, q - 1, 0))
    return um1 * w_ref[0:1, :] + u * w_ref[1:2, :] + up1 * w_ref[2:3, :] + b_ref[...]


def _conv3(ext, off, rows, w_ref, b_ref, keep_prev=1, keep_next=1):
    n = ext.shape[0]
    r = lax.broadcasted_iota(jnp.int32, (rows, n), 0)
    c = lax.broadcasted_iota(jnp.int32, (rows, n), 1)
    down = ((c == r + (off - 1)) & (r + keep_prev > 0)).astype(BF16)
    up = ((c == r + (off + 1)) & (r - keep_next < rows - 1)).astype(BF16)
    um1 = jnp.dot(down, ext, preferred_element_type=F32)
    up1 = jnp.dot(up, ext, preferred_element_type=F32)
    u = ext[off:off + rows].astype(F32)
    return um1 * w_ref[0:1, :] + u * w_ref[1:2, :] + up1 * w_ref[2:3, :] + b_ref[...]


def _ssd_body(nc, emit_y, xs_ref, xsp_ref, xsn_ref, bc_ref, bcp_ref, bcn_ref, dt_ref, z_ref,
              cwx_ref, cbx_ref, cwbc_ref, cbbc_ref, dtb_ref, a_ref, dexp_ref, ng_ref,
              e_ref, e2_ref, init_ref, o_ref, st_ref, ysc_ref):
    q = SSD_CHUNK
    p = pl.program_id(1)
    c = pl.program_id(2)
    is_fwd = p == 1
    ci = jnp.where(is_fwd, c, nc - 1 - c)

    @pl.when(c == 0)
    def _():
        st_ref[...] = init_ref[0, 0]

    keep_prev = (ci > 0).astype(jnp.int32)
    keep_next = (ci < nc - 1).astype(jnp.int32)
    halo = BF16_SUBLANES
    xs = _silu(_conv3(jnp.concatenate([xsp_ref[...], xs_ref[...], xsn_ref[...]], axis=0), halo, q,
                      cwx_ref, cbx_ref, keep_prev, keep_next))
    bc = _silu(_conv3(jnp.concatenate([bcp_ref[...], bc_ref[...], bcn_ref[...]], axis=0), halo, q,
                      cwbc_ref, cbbc_ref, keep_prev, keep_next))
    ng2 = SSM_GROUPS * SSM_STATE
    bm, cm = bc[:, :ng2], bc[:, ng2:]

    dtr = dt_ref[...] + dtb_ref[0]
    dt = jnp.maximum(dtr, 0.0) + jnp.log(1.0 + jnp.exp(-jnp.abs(dtr)))
    a = dt * a_ref[0]

    rr = lax.broadcasted_iota(jnp.int32, (q, q), 0)
    cc = lax.broadcasted_iota(jnp.int32, (q, q), 1)
    mask = (rr - cc) * jnp.where(is_fwd, 1, -1) >= 0
    tri = mask.astype(BF16)
    a_hi, a_lo = _split_bf16(a)
    acs2 = jnp.dot(tri, jnp.concatenate([a_hi, a_lo], axis=1), preferred_element_type=F32)
    acs = acs2[:, :LANES] + acs2[:, LANES:]
    c_hi, c_lo = _split_bf16(acs)
    acs_cat = jnp.concatenate([c_hi, c_lo], axis=1)
    ex = jnp.dot(acs_cat, e_ref[...], preferred_element_type=F32)
    d_hi, d_lo = _split_bf16(dt)
    dt_exp = jnp.dot(jnp.concatenate([d_hi, d_lo], axis=1), e_ref[...], preferred_element_type=F32)
    ex_end = jnp.where(is_fwd, ex[q - 1:q, :], ex[0:1, :])

    xdt = xs * dt_exp
    xw = (xdt * jnp.exp(ex_end - ex)).astype(BF16)
    state = st_ref[...]

    if emit_y:
        acs_t = acs.T
        cb_all = jnp.dot(acs_cat, e2_ref[...], preferred_element_type=F32)
        lane = lax.broadcasted_iota(jnp.int32, (q, LANES), 1)
        state_bf = state.astype(BF16)
        hpg = SSM_HEADS // SSM_GROUPS
        gw = hpg * SSM_HEAD_DIM
        pieces = []
        for g in range(SSM_GROUPS):
            cg = cm[:, g * SSM_STATE:(g + 1) * SSM_STATE].astype(BF16)
            bg = bm[:, g * SSM_STATE:(g + 1) * SSM_STATE].astype(BF16)
            cbg = lax.dot_general(cg, bg, (((1,), (1,)), ((), ())), preferred_element_type=F32)
            yoff = jnp.dot(cg, state_bf[:, g * gw:(g + 1) * gw], preferred_element_type=F32)
            yoff = yoff * jnp.exp(ex[:, g * gw:(g + 1) * gw])
            for j in range(hpg // 2):
                h0 = g * hpg + 2 * j
                ms = []
                for h in (h0, h0 + 1):
                    seg = cb_all[:, h * q:(h + 1) * q] - acs_t[h:h + 1, :]
                    ms.append((cbg * jnp.exp(jnp.where(mask, seg, -jnp.inf))).astype(BF16))
                m2 = jnp.concatenate(ms, axis=1)
                xp = xdt[:, h0 * SSM_HEAD_DIM:(h0 + 2) * SSM_HEAD_DIM]
                x2 = jnp.concatenate([jnp.where(lane < SSM_HEAD_DIM, xp, 0.0),
                                      jnp.where(lane >= SSM_HEAD_DIM, xp, 0.0)], axis=0).astype(BF16)
                yd = jnp.dot(m2, x2, preferred_element_type=F32)
                pieces.append(yd + yoff[:, 2 * j * SSM_HEAD_DIM:(2 * j + 2) * SSM_HEAD_DIM])
        y = jnp.concatenate(pieces, axis=1)
        row0 = pl.multiple_of(ci * q, q)

        @pl.when(p == 0)
        def _():
            ysc_ref[pl.ds(row0, q), :] = y

        @pl.when(p == 1)
        def _():
            yt = y + ysc_ref[pl.ds(row0, q), :] + xs * dexp_ref[...]
            v = yt * _silu(z_ref[...].astype(F32))
            outs = []
            for g in range(SSM_GROUPS):
                vg = v[:, g * gw:(g + 1) * gw]
                outs.append(vg * lax.rsqrt(jnp.mean(vg * vg, axis=-1, keepdims=True) + NORM_EPS))
            o_ref[...] = (jnp.concatenate(outs, axis=1) * ng_ref[...]).astype(o_ref.dtype)

    new_parts = []
    hpg = SSM_HEADS // SSM_GROUPS
    gw = hpg * SSM_HEAD_DIM
    for g in range(SSM_GROUPS):
        bt = bm[:, g * SSM_STATE:(g + 1) * SSM_STATE].T.astype(BF16)
        new_parts.append(jnp.dot(bt, xw[:, g * gw:(g + 1) * gw], preferred_element_type=F32))
    st_new = state * jnp.exp(ex_end) + jnp.concatenate(new_parts, axis=1)
    st_ref[...] = st_new
    if not emit_y:
        @pl.when(c == nc - 1)
        def _():
            o_ref[0, 0] = st_new


def _ssd(proj, dt_raw, init, prm, emit_y, xs_col, bc_col, z_col):
    b = init.shape[0]
    l = proj.shape[0] // b
    q = SSD_CHUNK
    nc = l // q
    hb = q // BF16_SUBLANES
    nh = l // BF16_SUBLANES
    dx = SSM_HEADS * SSM_HEAD_DIM
    dbc = 2 * SSM_GROUPS * SSM_STATE

    def ci_of(p, c):
        return p * c + (1 - p) * (nc - 1 - c)

    def main(col):
        return lambda i, p, c: (i * nc + ci_of(p, c), col)

    def prev(col):
        return lambda i, p, c: (i * nh + jnp.maximum(ci_of(p, c) * hb - 1, 0), col)

    def nxt(col):
        return lambda i, p, c: (i * nh + jnp.minimum(ci_of(p, c) * hb + hb, nh - 1), col)

    const2 = lambda i, p, c: (0, 0)
    bydir = lambda i, p, c: (1 - p, 0, 0)
    in_specs = [
        pl.BlockSpec((q, dx), main(xs_col)),
        pl.BlockSpec((BF16_SUBLANES, dx), prev(xs_col)),
        pl.BlockSpec((BF16_SUBLANES, dx), nxt(xs_col)),
        pl.BlockSpec((q, dbc), main(bc_col)),
        pl.BlockSpec((BF16_SUBLANES, dbc), prev(bc_col)),
        pl.BlockSpec((BF16_SUBLANES, dbc), nxt(bc_col)),
        pl.BlockSpec((q, LANES), lambda i, p, c: (i * nc + ci_of(p, c), 1 - p)),
        pl.BlockSpec((q, dx), (lambda i, p, c: (i * nc + c * p, z_col))),
        pl.BlockSpec((3, dx), const2), pl.BlockSpec((1, dx), const2),
        pl.BlockSpec((3, dbc), const2), pl.BlockSpec((1, dbc), const2),
        pl.BlockSpec((1, 1, LANES), bydir), pl.BlockSpec((1, 1, LANES), bydir),
        pl.BlockSpec((1, dx), const2), pl.BlockSpec((1, dx), const2),
        pl.BlockSpec((2 * LANES, dx), const2),
        pl.BlockSpec((2 * LANES, SSM_HEADS * q), const2),
        pl.BlockSpec((1, 1, SSM_STATE, dx), lambda i, p, c: (i, 1 - p, 0, 0)),
    ]
    if emit_y:
        out_spec = pl.BlockSpec((q, dx), lambda i, p, c: (i * nc + c * p, 0))
        out_shape = jax.ShapeDtypeStruct((b * l, dx), BF16)
    else:
        out_spec = pl.BlockSpec((1, 1, SSM_STATE, dx), lambda i, p, c: (i, 1 - p, 0, 0))
        out_shape = jax.ShapeDtypeStruct((b, 2, SSM_STATE, dx), F32)
    return pl.pallas_call(
        functools.partial(_ssd_body, nc, emit_y),
        grid=(b, 2, nc),
        in_specs=in_specs,
        out_specs=out_spec,
        out_shape=out_shape,
        scratch_shapes=[pltpu.VMEM((SSM_STATE, dx), F32),
                        pltpu.VMEM((l if emit_y else q, dx), F32)],
        compiler_params=_cparams(("parallel", "arbitrary", "arbitrary")),
        name="ssd_scan" if emit_y else "ssd_ctx_state",
    )(proj, proj, proj, proj, proj, proj, dt_raw, proj,
      prm["cwx"], prm["cbx"], prm["cwbc"], prm["cbbc"], prm["dtb"], prm["a"], prm["dexp"], prm["ng"],
      prm["e"], prm["e2"], init)


def _hyena_body(nfb, fb, x1_ref, x2_ref, v_ref, cw1_ref, cw2_ref, cw3_ref, cb1_ref, cb2_ref, cb3_ref,
                we_ref, wo_ref, wie_ref, wio_ref, coef_ref, skip_ref, kmid_ref, o_ref,
                zf_ref, ze_ref, zo_ref, acce_ref, acco_ref, lc_ref, x1c_ref, x2c_ref, mid_ref):
    o = pl.program_id(2)
    k = pl.program_id(3)
    ns, l, _ = zf_ref.shape
    ct = ns * LANES
    h = l // 2
    slab = lambda j: slice(j * LANES, (j + 1) * LANES)

    def split_z():
        kre = jnp.where(o == 0, kmid_ref[0:1, :], kmid_ref[2:3, :])
        kim = jnp.where(o == 0, kmid_ref[1:2, :], kmid_ref[3:4, :])
        j4 = lax.broadcasted_iota(jnp.int32, (8, LANES), 0) & 3
        for j in range(ns):
            ze_ref[:, slab(j)] = zf_ref[j, pl.ds(0, h, stride=2), :].astype(BF16)
            zo_ref[:, slab(j)] = zf_ref[j, pl.ds(1, h, stride=2), :].astype(BF16)
            z8 = jnp.sum(zf_ref[j].reshape(l // 8, 8, LANES), axis=0)
            ure = z8[0:1] + z8[4:5] - z8[2:3] - z8[6:7]
            uim = z8[3:4] + z8[7:8] - z8[1:2] - z8[5:6]
            yre = ure * kre[:, slab(j)] - uim * kim[:, slab(j)]
            yim = ure * kim[:, slab(j)] + uim * kre[:, slab(j)]
            pat = jnp.where(j4 == 0, yre, jnp.where(j4 == 1, -yim, jnp.where(j4 == 2, -yre, yim)))
            mid_ref[:, slab(j)] = pat * (1.0 / l)

    @pl.when((o == 0) & (k == 0))
    def _():
        rows = min(l, 256)
        halo = BF16_SUBLANES
        for r0 in range(0, l, rows):
            lo, hi = max(r0 - halo, 0), min(r0 + rows + halo, l)
            mid = slice(r0 - lo, r0 - lo + rows)
            x1c_ref[r0:r0 + rows, :] = _conv3_roll(x1_ref[lo:hi, :].astype(F32), cw1_ref, cb1_ref)[mid].astype(BF16)
            x2c_ref[r0:r0 + rows, :] = _conv3_roll(x2_ref[lo:hi, :].astype(F32), cw2_ref, cb2_ref)[mid].astype(BF16)
            vc = _conv3_roll(v_ref[lo:hi, :].astype(F32), cw3_ref, cb3_ref)[mid]
            for j in range(ns):
                zf_ref[j, r0:r0 + rows, :] = vc[:, slab(j)]
        split_z()

    @pl.when(k == 0)
    def _():
        acce_ref[...] = jnp.zeros_like(acce_ref)
        acco_ref[...] = jnp.zeros_like(acco_ref)

    half = ct // 2
    halves = (slice(0, half), slice(half, ct))
    spectra = []
    for cs in halves:
        a = jnp.dot(we_ref[...], ze_ref[:, cs], preferred_element_type=F32)
        b = jnp.dot(wo_ref[...], zo_ref[:, cs], preferred_element_type=F32)
        lre, lim = a[:fb] + b[:fb], a[fb:] + b[fb:]
        hre, him = a[:fb] - b[:fb], b[fb:] - a[fb:]
        c0, c1 = coef_ref[0, 0, 0, 0, :, cs], coef_ref[0, 0, 0, 1, :, cs]
        c2, c3 = coef_ref[0, 0, 0, 2, :, cs], coef_ref[0, 0, 0, 3, :, cs]
        ylre, ylim = lre * c0 - lim * c1, lre * c1 + lim * c0
        yhre, yhim = hre * c2 - him * c3, hre * c3 + him * c2
        spectra.append((jnp.concatenate([ylre + yhre, ylim - yhim], axis=0).astype(BF16),
                        jnp.concatenate([ylre - yhre, ylim + yhim], axis=0).astype(BF16)))
    for cs, (ge, go) in zip(halves, spectra):
        acce_ref[:, cs] += jnp.dot(wie_ref[0], ge, preferred_element_type=F32)
        acco_ref[:, cs] += jnp.dot(wio_ref[0], go, preferred_element_type=F32)

    @pl.when(k == nfb - 1)
    def _():
        skip = jnp.where(o == 0, skip_ref[0:1, :], skip_ref[1:2, :])

        def long_conv(j):
            lc_ref[j, pl.ds(0, h, stride=2), :] = acce_ref[:, slab(j)]
            lc_ref[j, pl.ds(1, h, stride=2), :] = acco_ref[:, slab(j)]
            lc = lc_ref[j] + zf_ref[j] * skip[:, slab(j)]
            return (lc.reshape(l // 8, 8, LANES) + mid_ref[:, slab(j)][None]).reshape(l, LANES)

        @pl.when(o == 0)
        def _():
            for j in range(ns):
                zf_ref[j] = x1c_ref[:, slab(j)].astype(F32) * long_conv(j)
            split_z()

        @pl.when(o == 1)
        def _():
            for j in range(ns):
                o_ref[:, slab(j)] = (x2c_ref[:, slab(j)].astype(F32) * long_conv(j)).astype(o_ref.dtype)


def _hyena(proj, b, cw, cb, tables, coef, kmid, skip, ct, fb):
    l = proj.shape[0] // b
    d = skip.shape[1]
    h = l // 2
    nct = d // ct
    nfb = h // fb
    we, wo, wie, wio = tables

    def col(part):
        return lambda j, i, o, k: (i, part * nct + j)

    def wcol(part):
        return lambda j, i, o, k: (0, part * nct + j)

    return pl.pallas_call(
        functools.partial(_hyena_body, nfb, fb),
        grid=(nct, b, HY_ORDER, nfb),
        in_specs=[pl.BlockSpec((l, ct), col(0)), pl.BlockSpec((l, ct), col(1)), pl.BlockSpec((l, ct), col(2)),
                  pl.BlockSpec((3, ct), wcol(0)), pl.BlockSpec((3, ct), wcol(1)), pl.BlockSpec((3, ct), wcol(2)),
                  pl.BlockSpec((1, ct), wcol(0)), pl.BlockSpec((1, ct), wcol(1)), pl.BlockSpec((1, ct), wcol(2)),
                  pl.BlockSpec((2 * fb, h), lambda j, i, o, k: (k, 0)),
                  pl.BlockSpec((2 * fb, h), lambda j, i, o, k: (k, 0)),
                  pl.BlockSpec((1, h, 2 * fb), lambda j, i, o, k: (k, 0, 0)),
                  pl.BlockSpec((1, h, 2 * fb), lambda j, i, o, k: (k, 0, 0)),
                  pl.BlockSpec((1, 1, 1, 4, fb, ct), lambda j, i, o, k: (o, j, k, 0, 0, 0)),
                  pl.BlockSpec((HY_ORDER, ct), lambda j, i, o, k: (0, j)),
                  pl.BlockSpec((2 * HY_ORDER, ct), lambda j, i, o, k: (0, j))],
        out_specs=pl.BlockSpec((l, ct), lambda j, i, o, k: (i, j)),
        out_shape=jax.ShapeDtypeStruct((b * l, d), BF16),
        scratch_shapes=[pltpu.VMEM((ct // LANES, l, LANES), F32), pltpu.VMEM((h, ct), BF16),
                        pltpu.VMEM((h, ct), BF16), pltpu.VMEM((h, ct), F32), pltpu.VMEM((h, ct), F32),
                        pltpu.VMEM((ct // LANES, l, LANES), F32),
                        pltpu.VMEM((l, ct), BF16), pltpu.VMEM((l, ct), BF16), pltpu.VMEM((8, ct), F32)],
        compiler_params=_cparams(("parallel", "parallel", "arbitrary", "arbitrary")),
        name="hyena",
    )(proj, proj, proj, cw, cw, cw, cb, cb, cb, we, wo, wie, wio, coef, skip, kmid)


def _cis_table(rows, cols, n):
    t = jnp.arange(cols, dtype=jnp.int32)
    digit = 16

    def cis(mult, count):
        j = jnp.arange(count, dtype=jnp.int32)
        ang = (2.0 * math.pi / n) * ((j[:, None] * mult * t[None, :]) % n).astype(F32)
        return jnp.cos(ang), jnp.sin(ang)

    def cmul(a, b):
        (ca, sa), (cb_, sb) = a, b
        c = ca[:, None] * cb_[None] - sa[:, None] * sb[None]
        s = sa[:, None] * cb_[None] + ca[:, None] * sb[None]
        return c.reshape(-1, cols), s.reshape(-1, cols)

    c, s = cmul(cis(digit * digit, -(-rows // (digit * digit))), cmul(cis(digit, digit), cis(1, digit)))
    return c[:rows], s[:rows]


def _dft_tables(l, fb):
    n, h = 2 * l, l // 2
    f = jnp.arange(h, dtype=F32)[:, None]
    ce, se = _cis_table(h, h, l)
    ct_, st_ = jnp.cos((2.0 * math.pi / n) * f), jnp.sin((2.0 * math.pi / n) * f)
    co, so = ce * ct_ - se * st_, se * ct_ + ce * st_
    cf = jnp.where(f == 0, 1.0 / n, 2.0 / n)

    def blocked(re, im):
        return jnp.concatenate([re.reshape(h // fb, 1, fb, h), im.reshape(h // fb, 1, fb, h)], axis=1).reshape(l, h)

    def inverse(c, s):
        return blocked(cf * c, -(2.0 / n) * s).reshape(h // fb, 2 * fb, h).transpose(0, 2, 1).astype(BF16)

    return (blocked(ce, -se).astype(BF16), blocked(co, -so).astype(BF16), inverse(ce, se), inverse(co, so))


def _implicit_filters(l, hp, parity):
    hi = lax.Precision.HIGHEST
    pos = jnp.arange(parity, l, 2, dtype=F32)
    t = pos / max(l - 1, 1)
    bands = jnp.linspace(1e-4, HY_BANDS - 1, HY_BANDS, dtype=F32)
    ang = (2.0 * math.pi / l) * pos[:, None] * bands[None, :]
    z = jnp.concatenate([t[:, None], jnp.cos(ang), -jnp.sin(ang)], axis=-1)
    freq = hp["hy_freq"]
    hdn = jnp.sin(freq[0] * (jnp.dot(z, hp["hy_w1"], precision=hi) + hp["hy_b1"]))
    hdn = jnp.sin(freq[1] * (jnp.dot(hdn, hp["hy_w2"], precision=hi) + hp["hy_b2"]))
    hdn = jnp.sin(freq[2] * (jnp.dot(hdn, hp["hy_w3"], precision=hi) + hp["hy_b3"]))
    d = hp["hy_decay"].shape[-1]
    h = jnp.dot(hdn, hp["hy_w_out"], precision=hi).reshape(pos.shape[0], 2, HY_ORDER, d)
    window = jnp.exp(-t[:, None, None] * jnp.abs(hp["hy_decay"]))
    return h * window[:, None]


def _filter_coefs(l, fb, ct, we, wo, hp):
    h = l // 2
    d = hp["hy_decay"].shape[-1]
    nct, nfb = d // ct, h // fb
    he = _implicit_filters(l, hp, 0).at[0, 1].set(0.0)
    ho = _implicit_filters(l, hp, 1)
    he2 = he.reshape(h, 2 * HY_ORDER * d)
    ho2 = ho.reshape(h, 2 * HY_ORDER * d)

    def body(we_ref, wo_ref, fe_ref, fo_ref, be_ref, bo_ref, o_ref):
        def spectrum(e_ref, o_ref_):
            a = jnp.dot(we_ref[...], e_ref[...], preferred_element_type=F32)
            b = jnp.dot(wo_ref[...], o_ref_[...], preferred_element_type=F32)
            return a[:fb] + b[:fb], a[fb:] + b[fb:], a[:fb] - b[:fb], b[fb:] - a[fb:]
        f_lre, f_lim, f_hre, f_him = spectrum(fe_ref, fo_ref)
        b_lre, b_lim, b_hre, b_him = spectrum(be_ref, bo_ref)
        o_ref[0, 0, 0, 0] = f_lre + b_lre
        o_ref[0, 0, 0, 1] = f_lim - b_lim
        o_ref[0, 0, 0, 2] = f_hre + b_hre
        o_ref[0, 0, 0, 3] = f_him - b_him

    def col(direction):
        return lambda o, j, k: (0, (direction * HY_ORDER + o) * nct + j)

    wspec = pl.BlockSpec((2 * fb, h), lambda o, j, k: (k, 0))
    coef = pl.pallas_call(
        body,
        grid=(HY_ORDER, nct, nfb),
        in_specs=[wspec, wspec, pl.BlockSpec((h, ct), col(0)), pl.BlockSpec((h, ct), col(0)),
                  pl.BlockSpec((h, ct), col(1)), pl.BlockSpec((h, ct), col(1))],
        out_specs=pl.BlockSpec((1, 1, 1, 4, fb, ct), lambda o, j, k: (o, j, k, 0, 0, 0)),
        out_shape=jax.ShapeDtypeStruct((HY_ORDER, nct, nfb, 4, fb, ct), F32),
        compiler_params=_cparams(("parallel", "parallel", "arbitrary")),
        name="filter_spectrum",
    )(we, wo, he2.astype(BF16), ho2.astype(BF16), he2.astype(BF16), ho2.astype(BF16))

    alt = jnp.where(jnp.arange(h) % 2 == 0, 1.0, -1.0).astype(F32)
    hi = lax.Precision.HIGHEST
    fre = jnp.dot(alt, he2, precision=hi).reshape(2, HY_ORDER, d)
    fim = -jnp.dot(alt, ho2, precision=hi).reshape(2, HY_ORDER, d)
    kmid = jnp.stack([fre[0] + fre[1], fim[0] - fim[1]], axis=1).reshape(2 * HY_ORDER, d)
    return coef, kmid


def _outproj_body(zhy_ref, yn_ref, ghy_ref, gssd_ref, x_ref, g1_ref, sh2_ref, sc2_ref, n2g_ref,
                  why_ref, wssm_ref, wo_ref, x1_ref, h2_ref, h2p_ref):
    hyo = jnp.dot(zhy_ref[...], why_ref[...], preferred_element_type=F32)
    sso = jnp.dot(yn_ref[...], wssm_ref[...], preferred_element_type=F32)
    merged = (jax.nn.sigmoid(ghy_ref[...].astype(F32)) * hyo
              + jax.nn.sigmoid(gssd_ref[...].astype(F32)) * sso)
    xmix = jnp.dot(merged.astype(BF16), wo_ref[...], preferred_element_type=F32)
    x1 = x_ref[0] + g1_ref[0] * xmix
    x1_ref[...] = x1
    r = lax.rsqrt(jnp.mean(x1 * x1, axis=-1, keepdims=True) + NORM_EPS)
    h2 = (x1 * r * n2g_ref[...]) * (1.0 + sc2_ref[0]) + sh2_ref[0]
    h2_ref[...] = h2
    h2p_ref[...] = _pack_bf16_pairs(h2)


def _outproj(zhy, yn, proj, x, gate1, shift2, scale2, n2g, why, wssm, wo, ghy_col, gssd_col):
    b, l, d = x.shape
    tm = min(l, 512)
    nl = l // tm
    tile = lambda col: pl.BlockSpec((tm, d), lambda i, j: (i * nl + j, col))
    row = pl.BlockSpec((1, 1, d), lambda i, j: (i, 0, 0))
    wspec = pl.BlockSpec((d, d), lambda i, j: (0, 0))
    return pl.pallas_call(
        _outproj_body,
        grid=(b, nl),
        in_specs=[tile(0), tile(0), tile(ghy_col), tile(gssd_col),
                  pl.BlockSpec((1, tm, d), lambda i, j: (i, j, 0)), row, row, row,
                  pl.BlockSpec((1, d), lambda i, j: (0, 0)), wspec, wspec, wspec],
        out_specs=[tile(0), tile(0), pl.BlockSpec((tm, d // 2), lambda i, j: (i * nl + j, 0))],
        out_shape=[jax.ShapeDtypeStruct((b * l, d), F32), jax.ShapeDtypeStruct((b * l, d), F32),
                   jax.ShapeDtypeStruct((b * l, d // 2), jnp.uint32)],
        compiler_params=_cparams(("parallel", "parallel")),
        name="merge_outproj",
    )(zhy, yn, proj, proj, x, gate1.reshape(b, 1, d), shift2.reshape(b, 1, d), scale2.reshape(b, 1, d),
      n2g.reshape(1, d), why, wssm, wo)


PER_GROUP = N_EXPERTS // N_EXPERT_GROUPS


def _route_body(tm, h_ref, wh_ref, wl_ref, bias_ref, e_ref, c_ref, w_ref, cnt_ref, carry_ref):
    ng, pg = N_EXPERT_GROUPS, PER_GROUP

    @pl.when(pl.program_id(0) == 0)
    def _():
        carry_ref[...] = jnp.zeros_like(carry_ref)

    x_hi, x_lo = _split_bf16(h_ref[...])
    nt = (((1,), (1,)), ((), ()))
    lg = (lax.dot_general(wh_ref[...], x_hi, nt, preferred_element_type=F32)
          + lax.dot_general(wh_ref[...], x_lo, nt, preferred_element_type=F32)
          + lax.dot_general(wl_ref[...], x_hi, nt, preferred_element_type=F32))
    sc = jax.nn.sigmoid(lg)
    sel = sc + bias_ref[...]

    mem = [sel[m * ng:(m + 1) * ng, :] for m in range(pg)]
    m1 = functools.reduce(jnp.maximum, mem)
    first = functools.reduce(jnp.minimum, [jnp.where(mem[m] == m1, m, pg) for m in range(pg)])
    m2 = functools.reduce(jnp.maximum, [jnp.where(first == m, -jnp.inf, mem[m]) for m in range(pg)])
    gs = m1 + m2
    gid = lax.broadcasted_iota(jnp.int32, gs.shape, 0)
    beats = jnp.zeros(gs.shape, jnp.int32)
    for g2 in range(ng):
        v = gs[g2:g2 + 1, :]
        beats = beats + ((v > gs) | ((v == gs) & (g2 < gid))).astype(jnp.int32)
    keep = beats < TOPK_GROUPS
    selm = jnp.concatenate([jnp.where(keep, mem[m], -jnp.inf) for m in range(pg)], axis=0)

    r = lax.broadcasted_iota(jnp.int32, selm.shape, 0)
    eid = (r % ng) * pg + r // ng
    rank = jnp.zeros(selm.shape, jnp.int32)
    for r2 in range(N_EXPERTS):
        v = selm[r2:r2 + 1, :]
        e2 = (r2 % ng) * pg + r2 // ng
        rank = rank + ((v > selm) | ((v == selm) & (e2 < eid))).astype(jnp.int32)
    chosen = rank < TOP_K
    wsel = jnp.where(chosen, sc, 0.0)
    wn = wsel / jnp.sum(wsel, axis=0, keepdims=True) * ROUTED_SCALE

    cb = chosen.astype(BF16)
    t0 = lax.broadcasted_iota(jnp.int32, (tm, tm), 0)
    t1 = lax.broadcasted_iota(jnp.int32, (tm, tm), 1)
    before = jnp.dot(cb, (t0 < t1).astype(BF16), preferred_element_type=F32) + carry_ref[...]
    carry = carry_ref[...] + jnp.dot(cb, jnp.ones((tm, tm), BF16), preferred_element_type=F32)
    carry_ref[...] = carry
    cnt_ref[...] = carry

    k8 = lax.broadcasted_iota(jnp.int32, e_ref.shape, 0)
    eidf = eid.astype(F32)
    e_out = jnp.zeros(e_ref.shape, F32)
    c_out = jnp.zeros(e_ref.shape, F32)
    w_out = jnp.zeros(e_ref.shape, F32)
    for k in range(TOP_K):
        mk = rank == k
        e_out = jnp.where(k8 == k, jnp.sum(jnp.where(mk, eidf, 0.0), axis=0, keepdims=True), e_out)
        c_out = jnp.where(k8 == k, jnp.sum(jnp.where(mk, before, 0.0), axis=0, keepdims=True), c_out)
        w_out = jnp.where(k8 == k, jnp.sum(jnp.where(mk, wn, 0.0), axis=0, keepdims=True), w_out)
    e_ref[...] = e_out.astype(jnp.int32)
    c_ref[...] = c_out.astype(jnp.int32)
    w_ref[...] = w_out


def _route(h2, router_w, router_bias):
    t, d = h2.shape
    tm = min(t, 256)
    ng, pg = N_EXPERT_GROUPS, PER_GROUP
    wt = router_w.T.reshape(ng, pg, d).transpose(1, 0, 2).reshape(N_EXPERTS, d)
    w_hi, w_lo = _split_bf16(wt)
    bias = jnp.broadcast_to(router_bias.astype(F32).reshape(ng, pg).T.reshape(N_EXPERTS, 1), (N_EXPERTS, tm))
    slot = pl.BlockSpec((8, tm), lambda i: (0, i))
    const = lambda shape: pl.BlockSpec(shape, lambda i: (0, 0))
    e8, c8, w8, cnt = pl.pallas_call(
        functools.partial(_route_body, tm),
        grid=(t // tm,),
        in_specs=[pl.BlockSpec((tm, d), lambda i: (i, 0)), const((N_EXPERTS, d)), const((N_EXPERTS, d)),
                  const((N_EXPERTS, tm))],
        out_specs=[slot, slot, slot, const((N_EXPERTS, tm))],
        out_shape=[jax.ShapeDtypeStruct((8, t), jnp.int32), jax.ShapeDtypeStruct((8, t), jnp.int32),
                   jax.ShapeDtypeStruct((8, t), F32), jax.ShapeDtypeStruct((N_EXPERTS, tm), F32)],
        scratch_shapes=[pltpu.VMEM((N_EXPERTS, tm), F32)],
        compiler_params=_cparams(("arbitrary",)),
        name="moe_route",
    )(h2, w_hi, w_lo, bias)
    counts = cnt[:, 0].reshape(pg, ng).T.reshape(N_EXPERTS).astype(jnp.int32)
    return e8[:TOP_K], c8[:TOP_K], w8[:TOP_K], counts


def _dispatch_plan(e6, c6, counts, n_slots):
    padded = (counts + EXPERT_BLOCK - 1) // EXPERT_BLOCK * EXPERT_BLOCK
    pad_end = jnp.cumsum(padded)
    pad_start = pad_end - padded
    nb = -(-n_slots // EXPERT_BLOCK) + N_EXPERTS
    blk = jnp.arange(nb, dtype=jnp.int32) * EXPERT_BLOCK
    block_e = jnp.minimum(jnp.sum((pad_end[None, :] <= blk[:, None]).astype(jnp.int32), axis=1), N_EXPERTS - 1)
    n_used = (pad_end[-1] // EXPERT_BLOCK).astype(jnp.int32).reshape(1)
    onehot = e6[..., None] == jnp.arange(N_EXPERTS, dtype=jnp.int32)
    pos = jnp.sum(jnp.where(onehot, pad_start, 0), axis=-1) + c6
    return pos.astype(jnp.int32), block_e.astype(jnp.int32), n_used, (pad_start + counts).astype(jnp.int32), \
        (padded - counts).astype(jnp.int32), nb


def _dispatch_body(tm, nt, nb, zs_ref, zn_ref, nu_ref, pos_ref, h_hbm, xs_hbm, buf_ref, zero_ref,
                   lsem_ref, sem_ref, zsem_ref):
    i = pl.program_id(0)
    slot = lax.rem(i, 3)
    nrow = TOP_K * tm

    def load(j, s):
        return pltpu.make_async_copy(h_hbm.at[pl.ds(pl.multiple_of(j * tm, tm), tm)], buf_ref.at[s], lsem_ref.at[s])

    @pl.when(i == 0)
    def _():
        load(0, 0).start()
        if nt > 1:
            load(1, 1).start()

    load(i, slot).wait()

    def body(r, carry):
        for k in range(TOP_K):
            pltpu.make_async_copy(buf_ref.at[slot, pl.ds(r, 1)],
                                  xs_hbm.at[pl.ds(pos_ref[0, 0, k * tm + r], 1)],
                                  sem_ref.at[slot]).start(priority=k % 2)
        return carry
    lax.fori_loop(0, tm, body, 0, unroll=8)

    def wait_batch(s):
        pltpu.make_async_copy(h_hbm.at[pl.ds(0, nrow)], xs_hbm.at[pl.ds(0, nrow)], sem_ref.at[s]).wait()

    nxt = lax.rem(i + 2, 3)

    @pl.when(i > 0)
    def _():
        wait_batch(nxt)

    @pl.when(i + 2 < nt)
    def _():
        load(i + 2, nxt).start()

    @pl.when(i == nt - 1)
    def _():
        wait_batch(slot)
        zero_ref[...] = jnp.zeros_like(zero_ref)

        def pad_copy(row):
            return pltpu.make_async_copy(zero_ref.at[pl.ds(0, 1)], xs_hbm.at[pl.ds(row, 1)], zsem_ref.at[0])

        def tail_copy(blk):
            row = pl.multiple_of(blk * EXPERT_BLOCK, EXPERT_BLOCK)
            return pltpu.make_async_copy(zero_ref, xs_hbm.at[pl.ds(row, EXPERT_BLOCK)], zsem_ref.at[1])

        def tail_start(blk, carry):
            tail_copy(blk).start()
            return carry
        lax.fori_loop(nu_ref[0], nb, tail_start, 0)

        def tail_wait(blk, carry):
            tail_copy(blk).wait()
            return carry
        lax.fori_loop(nu_ref[0], nb, tail_wait, 0)

        def per_expert(e, carry):
            def start(j, c2):
                pad_copy(zs_ref[e] + j).start()
                return c2
            lax.fori_loop(0, zn_ref[e], start, 0)

            def wait(j, c2):
                pad_copy(zs_ref[e] + j).wait()
                return c2
            lax.fori_loop(0, zn_ref[e], wait, 0)
            return carry
        lax.fori_loop(0, N_EXPERTS, per_expert, 0)


def _dispatch(h2, pos_t, zero_start, zero_n, n_used, nb, tm):
    t, d = h2.shape
    grid_spec = pltpu.PrefetchScalarGridSpec(
        num_scalar_prefetch=3,
        grid=(t // tm,),
        in_specs=[pl.BlockSpec((1, 1, TOP_K * tm), lambda i, zs, zn, nu: (i, 0, 0), memory_space=pltpu.SMEM),
                  pl.BlockSpec(memory_space=pl.ANY)],
        out_specs=pl.BlockSpec(memory_space=pl.ANY),
        scratch_shapes=[pltpu.VMEM((3, tm, d), h2.dtype), pltpu.VMEM((EXPERT_BLOCK, d), h2.dtype),
                        pltpu.SemaphoreType.DMA((3,)), pltpu.SemaphoreType.DMA((3,)),
                        pltpu.SemaphoreType.DMA((2,))],
    )
    return pl.pallas_call(
        functools.partial(_dispatch_body, tm, t // tm, nb),
        grid_spec=grid_spec,
        out_shape=jax.ShapeDtypeStruct((nb * EXPERT_BLOCK, d), h2.dtype),
        compiler_params=_cparams(("arbitrary",)),
        name="moe_dispatch",
    )(zero_start, zero_n, n_used, pos_t, h2)


def _experts_body(be_ref, nu_ref, x_ref, wg_ref, wu_ref, wd_ref, o_ref):
    used = pl.program_id(0) < nu_ref[0]

    @pl.when(used)
    def _():
        half = x_ref.shape[1]
        lo, hi = _unpack_bf16_pairs(x_ref[...])
        x_lo, x_hi = lo.astype(BF16), hi.astype(BF16)
        hg = (jnp.dot(x_lo, wg_ref[0, :half, :], preferred_element_type=F32)
              + jnp.dot(x_hi, wg_ref[0, half:, :], preferred_element_type=F32))
        hu = (jnp.dot(x_lo, wu_ref[0, :half, :], preferred_element_type=F32)
              + jnp.dot(x_hi, wu_ref[0, half:, :], preferred_element_type=F32))
        act = (_silu(hg) * hu).astype(BF16)
        o_ref[...] = _pack_bf16_pairs(jnp.dot(act, wd_ref[0], preferred_element_type=F32))

    @pl.when(jnp.logical_not(used))
    def _():
        o_ref[...] = jnp.zeros_like(o_ref)


def _experts(xs, block_e, n_used, wg, wu, wd):
    n_rows, dh = xs.shape
    d, de = wg.shape[1], wg.shape[2]
    rb = EXPERT_BLOCK
    last = lambda i, nu: jnp.minimum(i, nu[0] - 1)
    grid_spec = pltpu.PrefetchScalarGridSpec(
        num_scalar_prefetch=2,
        grid=(n_rows // rb,),
        in_specs=[pl.BlockSpec((rb, dh), lambda i, be, nu: (last(i, nu), 0)),
                  pl.BlockSpec((1, d, de), lambda i, be, nu: (be[last(i, nu)], 0, 0)),
                  pl.BlockSpec((1, d, de), lambda i, be, nu: (be[last(i, nu)], 0, 0)),
                  pl.BlockSpec((1, de, d), lambda i, be, nu: (be[last(i, nu)], 0, 0))],
        out_specs=pl.BlockSpec((rb, dh), lambda i, be, nu: (i, 0)),
    )
    return pl.pallas_call(
        _experts_body,
        grid_spec=grid_spec,
        out_shape=jax.ShapeDtypeStruct((n_rows, dh), jnp.uint32),
        compiler_params=_cparams(("arbitrary",)),
        name="experts",
    )(block_e, n_used, xs, wg, wu, wd)


def _combine_body(tm, pos_ref, posn_ref, ys_hbm, w_ref, h_ref, x1_ref, g2_ref, fg_ref,
                  sg_ref, su_ref, sd_ref, o_ref, buf_ref, sem_ref):
    i = pl.program_id(0)
    nt = pl.num_programs(0)
    slot = lax.rem(i, 2)
    nrow = TOP_K * tm

    def row_copy(src, r, s):
        return pltpu.make_async_copy(ys_hbm.at[pl.ds(src, 1)], buf_ref.at[s, pl.ds(r, 1)], sem_ref.at[s])

    def issue(idx_ref, s):
        def body(r, carry):
            for k in range(TOP_K):
                row_copy(idx_ref[0, 0, k * tm + r], k * tm + r, s).start(priority=k % 2)
            return carry
        lax.fori_loop(0, tm, body, 0, unroll=4)

    @pl.when(i == 0)
    def _():
        issue(pos_ref, 0)

    @pl.when(i + 1 < nt)
    def _():
        issue(posn_ref, 1 - slot)

    pltpu.make_async_copy(ys_hbm.at[pl.ds(0, nrow)], buf_ref.at[slot], sem_ref.at[slot]).wait()

    w = w_ref[...]
    half = buf_ref.shape[2]
    r_lo = jnp.zeros((tm, half), F32)
    r_hi = jnp.zeros((tm, half), F32)
    for k in range(TOP_K):
        lo, hi = _unpack_bf16_pairs(buf_ref[slot, k * tm:(k + 1) * tm, :])
        r_lo = r_lo + w[:, k:k + 1] * lo
        r_hi = r_hi + w[:, k:k + 1] * hi
    routed = jnp.concatenate([r_lo, r_hi], axis=1)
    xb = h_ref[...].astype(BF16)
    hg = jnp.dot(xb, sg_ref[...], preferred_element_type=F32)
    hu = jnp.dot(xb, su_ref[...], preferred_element_type=F32)
    shared = jnp.dot((_silu(hg) * hu).astype(BF16), sd_ref[...], preferred_element_type=F32)
    x2 = x1_ref[...] + g2_ref[0] * (routed + shared)
    r = lax.rsqrt(jnp.mean(x2 * x2, axis=-1, keepdims=True) + NORM_EPS)
    o_ref[0] = x2 * r * fg_ref[...]


def _tile_slots(pos, tm):
    k, t = pos.shape
    return pos.reshape(k, t // tm, tm).transpose(1, 0, 2).reshape(t // tm, 1, k * tm)


def _combine(ys, pos_t, w6, h2, x1, gate2, final_g, sg, su, sd, tokens_per_batch, tm):
    t, d = h2.shape
    nt = t // tm
    tpb = tokens_per_batch // tm
    ds = sg.shape[1]
    wpad = jnp.zeros((t, LANES), F32).at[:, :TOP_K].set(w6.T)
    tile = pl.BlockSpec((tm, d), lambda i: (i, 0))
    return pl.pallas_call(
        functools.partial(_combine_body, tm),
        grid=(nt,),
        in_specs=[pl.BlockSpec((1, 1, TOP_K * tm), lambda i: (i, 0, 0), memory_space=pltpu.SMEM),
                  pl.BlockSpec((1, 1, TOP_K * tm), lambda i: (jnp.minimum(i + 1, nt - 1), 0, 0),
                               memory_space=pltpu.SMEM),
                  pl.BlockSpec(memory_space=pl.ANY),
                  pl.BlockSpec((tm, LANES), lambda i: (i, 0)),
                  tile, tile,
                  pl.BlockSpec((1, 1, d), lambda i: (i // tpb, 0, 0)),
                  pl.BlockSpec((1, d), lambda i: (0, 0)),
                  pl.BlockSpec((d, ds), lambda i: (0, 0)),
                  pl.BlockSpec((d, ds), lambda i: (0, 0)),
                  pl.BlockSpec((ds, d), lambda i: (0, 0))],
        out_specs=pl.BlockSpec((1, tm, d), lambda i: (i // tpb, i % tpb, 0)),
        out_shape=jax.ShapeDtypeStruct((t // tokens_per_batch, tokens_per_batch, d), F32),
        scratch_shapes=[pltpu.VMEM((2, TOP_K * tm, d // 2), jnp.uint32), pltpu.SemaphoreType.DMA((2,))],
        compiler_params=_cparams(("arbitrary",)),
        name="moe_combine",
    )(pos_t, pos_t, ys, wpad, h2, x1, gate2, final_g.reshape(1, d), sg, su, sd)


def _ssd_params(ssm_conv_w, ssm_conv_b, ssm_dt_bias, ssm_a_log, ssm_d, ssm_norm_g):
    dx = SSM_HEADS * SSM_HEAD_DIM
    q = SSD_CHUNK
    pad = lambda v: jnp.zeros((2, 1, LANES), F32).at[:, 0, :SSM_HEADS].set(v)
    heads = jnp.arange(SSM_HEADS)
    e1 = (jnp.arange(dx)[None, :] // SSM_HEAD_DIM == heads[:, None]).astype(F32)
    e2 = (jnp.arange(SSM_HEADS * q)[None, :] // q == heads[:, None]).astype(F32)

    def stack(e):
        z = jnp.zeros((2 * LANES, e.shape[1]), F32)
        return z.at[:SSM_HEADS].set(e).at[LANES:LANES + SSM_HEADS].set(e).astype(BF16)

    return dict(
        cwx=ssm_conv_w[:, :dx], cbx=ssm_conv_b[None, :dx],
        cwbc=ssm_conv_w[:, dx:], cbbc=ssm_conv_b[None, dx:],
        dtb=pad(ssm_dt_bias), a=pad(-jnp.exp(ssm_a_log.astype(F32))),
        dexp=jnp.repeat(ssm_d, SSM_HEAD_DIM)[None, :], ng=ssm_norm_g[None, :],
        e=stack(e1), e2=stack(e2))


def kernel(x, c, ctx, c_ctx, ada_w, ada_b, norm1_g, norm2_g, w_in, ssm_conv_w, ssm_conv_b, ssm_dt_bias, ssm_a_log, ssm_d, ssm_norm_g, w_ssm_out, hy_conv_w, hy_conv_b, hy_w1, hy_b1, hy_w2, hy_b2, hy_w3, hy_b3, hy_freq, hy_w_out, hy_decay, hy_bias, w_hy_out, w_o, router_w, router_bias, e_w_gate, e_w_up, e_w_down, sh_w_gate, sh_w_up, sh_w_down, final_g):
    depth = ada_w.shape[0]
    assert depth == 1, "single-layer block"
    b, l, d = x.shape
    lc = ctx.shape[1]
    dx = SSM_HEADS * SSM_HEAD_DIM
    dbc = 2 * SSM_GROUPS * SSM_STATE
    assert dx == d and hy_bias.shape[-1] == d

    rows = -(-(b + 1) // BF16_SUBLANES) * BF16_SUBLANES
    cc = jnp.zeros((rows, d), F32).at[:b].set(c).at[b].set(c_ctx)
    mod_all = _matmul(_silu(cc).astype(BF16), ada_w[0].astype(BF16), F32, rows, 512, "adaln") + ada_b[0]
    mod = mod_all[:b].reshape(b, 6, d)
    mod_c = jnp.broadcast_to(mod_all[b].reshape(1, 6, d), (b, 6, d))

    w = w_in[0]
    o_dt, o_z, o_hy, o_g = dx + dbc, dx + dbc + 2 * SSM_HEADS, 2 * dx + dbc + 2 * SSM_HEADS, 2 * dx + dbc + 2 * SSM_HEADS + 3 * d
    w_main = jnp.concatenate([w[:, o_hy:o_g], w[:, o_g:], w[:, o_z:o_hy], w[:, :dx + dbc]], axis=1).astype(BF16)
    w_dt = (jnp.zeros((d, 2 * LANES), F32)
            .at[:, :SSM_HEADS].set(w[:, o_dt:o_dt + SSM_HEADS])
            .at[:, LANES:LANES + SSM_HEADS].set(w[:, o_dt + SSM_HEADS:o_z])).astype(BF16)
    col_ghy, col_gssd, col_z, col_xs, col_bc = 3, 4, 5, 6, 7 * (d // dbc)

    sp = _ssd_params(ssm_conv_w[0], ssm_conv_b[0], ssm_dt_bias[0], ssm_a_log[0], ssm_d[0], ssm_norm_g[0])

    w_ctx = w_main[:, col_xs * d:]
    proj_c, dt_c = _norm_proj(ctx, norm1_g[0], mod_c[:, 0], mod_c[:, 1], w_ctx, w_dt, 1024, 512, "ctx_proj")
    zero_state = jnp.zeros((b, 2, SSM_STATE, dx), F32)
    ctx_state = _ssd(proj_c, dt_c, zero_state, sp, False, 0, d // dbc, 0)

    proj, dt_raw = _norm_proj(x, norm1_g[0], mod[:, 0], mod[:, 1], w_main, w_dt, 1024, 1536, "in_proj")
    yn = _ssd(proj, dt_raw, ctx_state, sp, True, col_xs, col_bc, col_z)

    fb = min(l // 2, 256)
    ct = 512
    tables = _dft_tables(l, fb)
    hp = dict(hy_w1=hy_w1[0], hy_b1=hy_b1[0], hy_w2=hy_w2[0], hy_b2=hy_b2[0], hy_w3=hy_w3[0], hy_b3=hy_b3[0],
              hy_freq=hy_freq[0], hy_w_out=hy_w_out[0], hy_decay=hy_decay[0])
    coef, kmid = _filter_coefs(l, fb, ct, tables[0], tables[1], hp)
    zhy = _hyena(proj, b, hy_conv_w[0], hy_conv_b[0][None, :], tables, coef, kmid, hy_bias[0], ct, fb)

    x1, h2, h2p = _outproj(zhy, yn, proj, x, mod[:, 2], mod[:, 3], mod[:, 4], norm2_g[0],
                      w_hy_out[0].astype(BF16), w_ssm_out[0].astype(BF16), w_o[0].astype(BF16),
                      col_ghy, col_gssd)

    e6, c6, w6, counts = _route(h2, router_w[0], router_bias[0])
    pos, block_e, n_used, zero_start, zero_n, nb = _dispatch_plan(e6, c6, counts, b * l * TOP_K)
    tm = min(l, 128)
    pos_t = _tile_slots(pos, tm)
    xs = _dispatch(h2p, pos_t, zero_start, zero_n, n_used, nb, tm)
    ys = _experts(xs, block_e, n_used, e_w_gate[0].astype(BF16), e_w_up[0].astype(BF16),
                  e_w_down[0].astype(BF16))
    return _combine(ys, pos_t, w6, h2, x1, mod[:, 5].reshape(b, 1, d), final_g,
                    sh_w_gate[0].astype(BF16), sh_w_up[0].astype(BF16), sh_w_down[0].astype(BF16), l, tm)
```

```python
import functools
import math

import jax
import jax.numpy as jnp
from jax import lax
from jax.experimental import pallas as pl
from jax.experimental.pallas import tpu as pltpu

F32 = jnp.float32
BF16 = jnp.bfloat16

NORM_EPS = 1e-6
SSM_HEADS = 16
SSM_HEAD_DIM = 64
SSM_GROUPS = 2
SSM_STATE = 128
SSD_CHUNK = 128
HY_ORDER = 2
HY_BANDS = 16
N_EXPERTS = 64
TOP_K = 6
N_EXPERT_GROUPS = 8
TOPK_GROUPS = 4
ROUTED_SCALE = 2.5
EXPERT_BLOCK = 512

LANES = 128
BF16_SUBLANES = 16
VMEM_LIMIT = 56 * 1024 * 1024


def _cparams(sem):
    return pltpu.CompilerParams(dimension_semantics=sem, vmem_limit_bytes=VMEM_LIMIT)


def _silu(x):
    return x * jax.nn.sigmoid(x)


def _pack_bf16_pairs(x):
    half = x.shape[1] // 2
    u = pltpu.bitcast(x.astype(BF16).astype(F32), jnp.uint32)
    return u[:, half:] | (u[:, :half] >> 16)


def _unpack_bf16_pairs(v):
    lo = pltpu.bitcast(v << 16, F32)
    hi = pltpu.bitcast(v & jnp.uint32(0xFFFF0000), F32)
    return lo, hi


def _split_bf16(q):
    hi = q.astype(BF16)
    lo = (q - hi.astype(F32)).astype(BF16)
    return hi, lo


def _norm_proj_body(x_ref, g_ref, sh_ref, sc_ref, w_ref, wdt_ref, o_ref, dt_ref, hx_ref):
    @pl.when(pl.program_id(1) == 0)
    def _():
        x = x_ref[0]
        r = lax.rsqrt(jnp.mean(x * x, axis=-1, keepdims=True) + NORM_EPS)
        hx = ((x * r * g_ref[...]) * (1.0 + sc_ref[0]) + sh_ref[0]).astype(BF16)
        hx_ref[...] = hx
        dt_ref[...] = jnp.dot(hx, wdt_ref[...], preferred_element_type=F32)

    o_ref[...] = jnp.dot(hx_ref[...], w_ref[...], preferred_element_type=F32).astype(o_ref.dtype)


def _norm_proj(x, g, shift, scale, w, w_dt, tm, tn, name):
    b, l, d = x.shape
    n, ndt = w.shape[1], w_dt.shape[1]
    tm, tn = min(tm, l), min(tn, n)
    assert l % tm == 0 and n % tn == 0, (l, n, tm, tn)
    npb = l // tm
    row = pl.BlockSpec((1, 1, d), lambda i, j: (i // npb, 0, 0))
    return pl.pallas_call(
        _norm_proj_body,
        grid=(b * npb, n // tn),
        in_specs=[pl.BlockSpec((1, tm, d), lambda i, j: (i // npb, i % npb, 0)),
                  pl.BlockSpec((1, d), lambda i, j: (0, 0)), row, row,
                  pl.BlockSpec((d, tn), lambda i, j: (0, j)),
                  pl.BlockSpec((d, ndt), lambda i, j: (0, 0))],
        out_specs=[pl.BlockSpec((tm, tn), lambda i, j: (i, j)), pl.BlockSpec((tm, ndt), lambda i, j: (i, 0))],
        out_shape=[jax.ShapeDtypeStruct((b * l, n), BF16), jax.ShapeDtypeStruct((b * l, ndt), F32)],
        scratch_shapes=[pltpu.VMEM((tm, d), BF16)],
        compiler_params=_cparams(("parallel", "arbitrary")),
        name=name,
    )(x, g.reshape(1, d), shift.reshape(b, 1, d), scale.reshape(b, 1, d), w, w_dt)


def _mm_body(a_ref, w_ref, o_ref):
    o_ref[...] = jnp.dot(a_ref[...], w_ref[...], preferred_element_type=F32).astype(o_ref.dtype)


def _matmul(a, w, out_dtype, tm, tn, name):
    m, k = a.shape
    n = w.shape[1]
    tm, tn = min(tm, m), min(tn, n)
    assert m % tm == 0 and n % tn == 0, (m, n, tm, tn)
    return pl.pallas_call(
        _mm_body,
        grid=(m // tm, n // tn),
        in_specs=[pl.BlockSpec((tm, k), lambda i, j: (i, 0)),
                  pl.BlockSpec((k, tn), lambda i, j: (0, j))],
        out_specs=pl.BlockSpec((tm, tn), lambda i, j: (i, j)),
        out_shape=jax.ShapeDtypeStruct((m, n), out_dtype),
        compiler_params=_cparams(("parallel", "arbitrary")),
        name=name,
    )(a, w)


def _conv3_roll(u, w_ref, b_ref):
    q = u.shape[0]
    row = lax.broadcasted_iota(jnp.int32, u.shape, 0)
    um1 = jnp.where(row == 0, 0.0, pltpu.roll(u, 1, 0))
    up1 = jnp.where(row == q - 1, 0.0, pltpu.roll(u, q - 1, 0))
    return um1 * w_ref[0:1, :] + u * w_ref[1:2, :] + up1 * w_ref[2:3, :] + b_ref[...]


def _conv3(ext, off, rows, w_ref, b_ref, keep_prev=1, keep_next=1):
    n = ext.shape[0]
    r = lax.broadcasted_iota(jnp.int32, (rows, n), 0)
    c = lax.broadcasted_iota(jnp.int32, (rows, n), 1)
    down = ((c == r + (off - 1)) & (r + keep_prev > 0)).astype(BF16)
    up = ((c == r + (off + 1)) & (r - keep_next < rows - 1)).astype(BF16)
    um1 = jnp.dot(down, ext, preferred_element_type=F32)
    up1 = jnp.dot(up, ext, preferred_element_type=F32)
    u = ext[off:off + rows].astype(F32)
    return um1 * w_ref[0:1, :] + u * w_ref[1:2, :] + up1 * w_ref[2:3, :] + b_ref[...]


def _ssd_body(nc, emit_y, xs_ref, xsp_ref, xsn_ref, bc_ref, bcp_ref, bcn_ref, dt_ref, z_ref,
              cwx_ref, cbx_ref, cwbc_ref, cbbc_ref, dtb_ref, a_ref, dexp_ref, ng_ref,
              e_ref, e2_ref, init_ref, o_ref, st_ref, ysc_ref):
    q = SSD_CHUNK
    p = pl.program_id(1)
    c = pl.program_id(2)
    is_fwd = p == 1
    ci = jnp.where(is_fwd, c, nc - 1 - c)

    @pl.when(c == 0)
    def _():
        st_ref[...] = init_ref[0, 0]

    keep_prev = (ci > 0).astype(jnp.int32)
    keep_next = (ci < nc - 1).astype(jnp.int32)
    halo = BF16_SUBLANES
    xs = _silu(_conv3(jnp.concatenate([xsp_ref[...], xs_ref[...], xsn_ref[...]], axis=0), halo, q,
                      cwx_ref, cbx_ref, keep_prev, keep_next))
    bc = _silu(_conv3(jnp.concatenate([bcp_ref[...], bc_ref[...], bcn_ref[...]], axis=0), halo, q,
                      cwbc_ref, cbbc_ref, keep_prev, keep_next))
    ng2 = SSM_GROUPS * SSM_STATE
    bm, cm = bc[:, :ng2], bc[:, ng2:]

    dtr = dt_ref[...] + dtb_ref[0]
    dt = jnp.maximum(dtr, 0.0) + jnp.log(1.0 + jnp.exp(-jnp.abs(dtr)))
    a = dt * a_ref[0]

    rr = lax.broadcasted_iota(jnp.int32, (q, q), 0)
    cc = lax.broadcasted_iota(jnp.int32, (q, q), 1)
    mask = (rr - cc) * jnp.where(is_fwd, 1, -1) >= 0
    tri = mask.astype(BF16)
    a_hi, a_lo = _split_bf16(a)
    acs2 = jnp.dot(tri, jnp.concatenate([a_hi, a_lo], axis=1), preferred_element_type=F32)
    acs = acs2[:, :LANES] + acs2[:, LANES:]
    c_hi, c_lo = _split_bf16(acs)
    acs_cat = jnp.concatenate([c_hi, c_lo], axis=1)
    ex = jnp.dot(acs_cat, e_ref[...], preferred_element_type=F32)
    d_hi, d_lo = _split_bf16(dt)
    dt_exp = jnp.dot(jnp.concatenate([d_hi, d_lo], axis=1), e_ref[...], preferred_element_type=F32)
    ex_end = jnp.where(is_fwd, ex[q - 1:q, :], ex[0:1, :])

    xdt = xs * dt_exp
    xw = (xdt * jnp.exp(ex_end - ex)).astype(BF16)
    state = st_ref[...]

    if emit_y:
        acs_t = acs.T
        cb_all = jnp.dot(acs_cat, e2_ref[...], preferred_element_type=F32)
        lane = lax.broadcasted_iota(jnp.int32, (q, LANES), 1)
        state_bf = state.astype(BF16)
        hpg = SSM_HEADS // SSM_GROUPS
        gw = hpg * SSM_HEAD_DIM
        pieces = []
        for g in range(SSM_GROUPS):
            cg = cm[:, g * SSM_STATE:(g + 1) * SSM_STATE].astype(BF16)
            bg = bm[:, g * SSM_STATE:(g + 1) * SSM_STATE].astype(BF16)
            cbg = lax.dot_general(cg, bg, (((1,), (1,)), ((), ())), preferred_element_type=F32)
            yoff = jnp.dot(cg, state_bf[:, g * gw:(g + 1) * gw], preferred_element_type=F32)
            yoff = yoff * jnp.exp(ex[:, g * gw:(g + 1) * gw])
            for j in range(hpg // 2):
                h0 = g * hpg + 2 * j
                ms = []
                for h in (h0, h0 + 1):
                    seg = cb_all[:, h * q:(h + 1) * q] - acs_t[h:h + 1, :]
                    ms.append((cbg * jnp.exp(jnp.where(mask, seg, -jnp.inf))).astype(BF16))
                m2 = jnp.concatenate(ms, axis=1)
                xp = xdt[:, h0 * SSM_HEAD_DIM:(h0 + 2) * SSM_HEAD_DIM]
                x2 = jnp.concatenate([jnp.where(lane < SSM_HEAD_DIM, xp, 0.0),
                                      jnp.where(lane >= SSM_HEAD_DIM, xp, 0.0)], axis=0).astype(BF16)
                yd = jnp.dot(m2, x2, preferred_element_type=F32)
                pieces.append(yd + yoff[:, 2 * j * SSM_HEAD_DIM:(2 * j + 2) * SSM_HEAD_DIM])
        y = jnp.concatenate(pieces, axis=1)
        row0 = pl.multiple_of(ci * q, q)

        @pl.when(p == 0)
        def _():
            ysc_ref[pl.ds(row0, q), :] = y

        @pl.when(p == 1)
        def _():
            yt = y + ysc_ref[pl.ds(row0, q), :] + xs * dexp_ref[...]
            v = yt * _silu(z_ref[...].astype(F32))
            outs = []
            for g in range(SSM_GROUPS):
                vg = v[:, g * gw:(g + 1) * gw]
                outs.append(vg * lax.rsqrt(jnp.mean(vg * vg, axis=-1, keepdims=True) + NORM_EPS))
            o_ref[...] = (jnp.concatenate(outs, axis=1) * ng_ref[...]).astype(o_ref.dtype)

    new_parts = []
    hpg = SSM_HEADS // SSM_GROUPS
    gw = hpg * SSM_HEAD_DIM
    for g in range(SSM_GROUPS):
        bt = bm[:, g * SSM_STATE:(g + 1) * SSM_STATE].T.astype(BF16)
        new_parts.append(jnp.dot(bt, xw[:, g * gw:(g + 1) * gw], preferred_element_type=F32))
    st_new = state * jnp.exp(ex_end) + jnp.concatenate(new_parts, axis=1)
    st_ref[...] = st_new
    if not emit_y:
        @pl.when(c == nc - 1)
        def _():
            o_ref[0, 0] = st_new


def _ssd(proj, dt_raw, init, prm, emit_y, xs_col, bc_col, z_col):
    b = init.shape[0]
    l = proj.shape[0] // b
    q = SSD_CHUNK
    nc = l // q
    hb = q // BF16_SUBLANES
    nh = l // BF16_SUBLANES
    dx = SSM_HEADS * SSM_HEAD_DIM
    dbc = 2 * SSM_GROUPS * SSM_STATE

    def ci_of(p, c):
        return p * c + (1 - p) * (nc - 1 - c)

    def main(col):
        return lambda i, p, c: (i * nc + ci_of(p, c), col)

    def prev(col):
        return lambda i, p, c: (i * nh + jnp.maximum(ci_of(p, c) * hb - 1, 0), col)

    def nxt(col):
        return lambda i, p, c: (i * nh + jnp.minimum(ci_of(p, c) * hb + hb, nh - 1), col)

    const2 = lambda i, p, c: (0, 0)
    bydir = lambda i, p, c: (1 - p, 0, 0)
    in_specs = [
        pl.BlockSpec((q, dx), main(xs_col)),
        pl.BlockSpec((BF16_SUBLANES, dx), prev(xs_col)),
        pl.BlockSpec((BF16_SUBLANES, dx), nxt(xs_col)),
        pl.BlockSpec((q, dbc), main(bc_col)),
        pl.BlockSpec((BF16_SUBLANES, dbc), prev(bc_col)),
        pl.BlockSpec((BF16_SUBLANES, dbc), nxt(bc_col)),
        pl.BlockSpec((q, LANES), lambda i, p, c: (i * nc + ci_of(p, c), 1 - p)),
        pl.BlockSpec((q, dx), (lambda i, p, c: (i * nc + c * p, z_col))),
        pl.BlockSpec((3, dx), const2), pl.BlockSpec((1, dx), const2),
        pl.BlockSpec((3, dbc), const2), pl.BlockSpec((1, dbc), const2),
        pl.BlockSpec((1, 1, LANES), bydir), pl.BlockSpec((1, 1, LANES), bydir),
        pl.BlockSpec((1, dx), const2), pl.BlockSpec((1, dx), const2),
        pl.BlockSpec((2 * LANES, dx), const2),
        pl.BlockSpec((2 * LANES, SSM_HEADS * q), const2),
        pl.BlockSpec((1, 1, SSM_STATE, dx), lambda i, p, c: (i, 1 - p, 0, 0)),
    ]
    if emit_y:
        out_spec = pl.BlockSpec((q, dx), lambda i, p, c: (i * nc + c * p, 0))
        out_shape = jax.ShapeDtypeStruct((b * l, dx), BF16)
    else:
        out_spec = pl.BlockSpec((1, 1, SSM_STATE, dx), lambda i, p, c: (i, 1 - p, 0, 0))
        out_shape = jax.ShapeDtypeStruct((b, 2, SSM_STATE, dx), F32)
    return pl.pallas_call(
        functools.partial(_ssd_body, nc, emit_y),
        grid=(b, 2, nc),
        in_specs=in_specs,
        out_specs=out_spec,
        out_shape=out_shape,
        scratch_shapes=[pltpu.VMEM((SSM_STATE, dx), F32),
                        pltpu.VMEM((l if emit_y else q, dx), F32)],
        compiler_params=_cparams(("parallel", "arbitrary", "arbitrary")),
        name="ssd_scan" if emit_y else "ssd_ctx_state",
    )(proj, proj, proj, proj, proj, proj, dt_raw, proj,
      prm["cwx"], prm["cbx"], prm["cwbc"], prm["cbbc"], prm["dtb"], prm["a"], prm["dexp"], prm["ng"],
      prm["e"], prm["e2"], init)


def _hyena_body(nfb, fb, x1_ref, x2_ref, v_ref, cw1_ref, cw2_ref, cw3_ref, cb1_ref, cb2_ref, cb3_ref,
                we_ref, wo_ref, wie_ref, wio_ref, coef_ref, skip_ref, kmid_ref, o_ref,
                zf_ref, ze_ref, zo_ref, acce_ref, acco_ref, lc_ref, x1c_ref, x2c_ref, mid_ref):
    o = pl.program_id(2)
    k = pl.program_id(3)
    ns, l, _ = zf_ref.shape
    ct = ns * LANES
    h = l // 2
    slab = lambda j: slice(j * LANES, (j + 1) * LANES)

    def split_z():
        kre = jnp.where(o == 0, kmid_ref[0:1, :], kmid_ref[2:3, :])
        kim = jnp.where(o == 0, kmid_ref[1:2, :], kmid_ref[3:4, :])
        j4 = lax.broadcasted_iota(jnp.int32, (8, LANES), 0) & 3
        for j in range(ns):
            ze_ref[:, slab(j)] = zf_ref[j, pl.ds(0, h, stride=2), :].astype(BF16)
            zo_ref[:, slab(j)] = zf_ref[j, pl.ds(1, h, stride=2), :].astype(BF16)
            z8 = jnp.sum(zf_ref[j].reshape(l // 8, 8, LANES), axis=0)
            ure = z8[0:1] + z8[4:5] - z8[2:3] - z8[6:7]
            uim = z8[3:4] + z8[7:8] - z8[1:2] - z8[5:6]
            yre = ure * kre[:, slab(j)] - uim * kim[:, slab(j)]
            yim = ure * kim[:, slab(j)] + uim * kre[:, slab(j)]
            pat = jnp.where(j4 == 0, yre, jnp.where(j4 == 1, -yim, jnp.where(j4 == 2, -yre, yim)))
            mid_ref[:, slab(j)] = pat * (1.0 / l)

    @pl.when((o == 0) & (k == 0))
    def _():
        rows = min(l, 256)
        halo = BF16_SUBLANES
        for r0 in range(0, l, rows):
            lo, hi = max(r0 - halo, 0), min(r0 + rows + halo, l)
            mid = slice(r0 - lo, r0 - lo + rows)
            x1c_ref[r0:r0 + rows, :] = _conv3_roll(x1_ref[lo:hi, :].astype(F32), cw1_ref, cb1_ref)[mid].astype(BF16)
            x2c_ref[r0:r0 + rows, :] = _conv3_roll(x2_ref[lo:hi, :].astype(F32), cw2_ref, cb2_ref)[mid].astype(BF16)
            vc = _conv3_roll(v_ref[lo:hi, :].astype(F32), cw3_ref, cb3_ref)[mid]
            for j in range(ns):
                zf_ref[j, r0:r0 + rows, :] = vc[:, slab(j)]
        split_z()

    @pl.when(k == 0)
    def _():
        acce_ref[...] = jnp.zeros_like(acce_ref)
        acco_ref[...] = jnp.zeros_like(acco_ref)

    half = ct // 2
    halves = (slice(0, half), slice(half, ct))
    spectra = []
    for cs in halves:
        a = jnp.dot(we_ref[...], ze_ref[:, cs], preferred_element_type=F32)
        b = jnp.dot(wo_ref[...], zo_ref[:, cs], preferred_element_type=F32)
        lre, lim = a[:fb] + b[:fb], a[fb:] + b[fb:]
        hre, him = a[:fb] - b[:fb], b[fb:] - a[fb:]
        c0, c1 = coef_ref[0, 0, 0, 0, :, cs], coef_ref[0, 0, 0, 1, :, cs]
        c2, c3 = coef_ref[0, 0, 0, 2, :, cs], coef_ref[0, 0, 0, 3, :, cs]
        ylre, ylim = lre * c0 - lim * c1, lre * c1 + lim * c0
        yhre, yhim = hre * c2 - him * c3, hre * c3 + him * c2
        spectra.append((jnp.concatenate([ylre + yhre, ylim - yhim], axis=0).astype(BF16),
                        jnp.concatenate([ylre - yhre, ylim + yhim], axis=0).astype(BF16)))
    for cs, (ge, go) in zip(halves, spectra):
        acce_ref[:, cs] += jnp.dot(wie_ref[0], ge, preferred_element_type=F32)
        acco_ref[:, cs] += jnp.dot(wio_ref[0], go, preferred_element_type=F32)

    @pl.when(k == nfb - 1)
    def _():
        skip = jnp.where(o == 0, skip_ref[0:1, :], skip_ref[1:2, :])

        def long_conv(j):
            lc_ref[j, pl.ds(0, h, stride=2), :] = acce_ref[:, slab(j)]
            lc_ref[j, pl.ds(1, h, stride=2), :] = acco_ref[:, slab(j)]
            lc = lc_ref[j] + zf_ref[j] * skip[:, slab(j)]
            return (lc.reshape(l // 8, 8, LANES) + mid_ref[:, slab(j)][None]).reshape(l, LANES)

        @pl.when(o == 0)
        def _():
            for j in range(ns):
                zf_ref[j] = x1c_ref[:, slab(j)].astype(F32) * long_conv(j)
            split_z()

        @pl.when(o == 1)
        def _():
            for j in range(ns):
                o_ref[:, slab(j)] = (x2c_ref[:, slab(j)].astype(F32) * long_conv(j)).astype(o_ref.dtype)


def _hyena(proj, b, cw, cb, tables, coef, kmid, skip, ct, fb):
    l = proj.shape[0] // b
    d = skip.shape[1]
    h = l // 2
    nct = d // ct
    nfb = h // fb
    we, wo, wie, wio = tables

    def col(part):
        return lambda j, i, o, k: (i, part * nct + j)

    def wcol(part):
        return lambda j, i, o, k: (0, part * nct + j)

    return pl.pallas_call(
        functools.partial(_hyena_body, nfb, fb),
        grid=(nct, b, HY_ORDER, nfb),
        in_specs=[pl.BlockSpec((l, ct), col(0)), pl.BlockSpec((l, ct), col(1)), pl.BlockSpec((l, ct), col(2)),
                  pl.BlockSpec((3, ct), wcol(0)), pl.BlockSpec((3, ct), wcol(1)), pl.BlockSpec((3, ct), wcol(2)),
                  pl.BlockSpec((1, ct), wcol(0)), pl.BlockSpec((1, ct), wcol(1)), pl.BlockSpec((1, ct), wcol(2)),
                  pl.BlockSpec((2 * fb, h), lambda j, i, o, k: (k, 0)),
                  pl.BlockSpec((2 * fb, h), lambda j, i, o, k: (k, 0)),
                  pl.BlockSpec((1, h, 2 * fb), lambda j, i, o, k: (k, 0, 0)),
                  pl.BlockSpec((1, h, 2 * fb), lambda j, i, o, k: (k, 0, 0)),
                  pl.BlockSpec((1, 1, 1, 4, fb, ct), lambda j, i, o, k: (o, j, k, 0, 0, 0)),
                  pl.BlockSpec((HY_ORDER, ct), lambda j, i, o, k: (0, j)),
                  pl.BlockSpec((2 * HY_ORDER, ct), lambda j, i, o, k: (0, j))],
        out_specs=pl.BlockSpec((l, ct), lambda j, i, o, k: (i, j)),
        out_shape=jax.ShapeDtypeStruct((b * l, d), BF16),
        scratch_shapes=[pltpu.VMEM((ct // LANES, l, LANES), F32), pltpu.VMEM((h, ct), BF16),
                        pltpu.VMEM((h, ct), BF16), pltpu.VMEM((h, ct), F32), pltpu.VMEM((h, ct), F32),
                        pltpu.VMEM((ct // LANES, l, LANES), F32),
                        pltpu.VMEM((l, ct), BF16), pltpu.VMEM((l, ct), BF16), pltpu.VMEM((8, ct), F32)],
        compiler_params=_cparams(("parallel", "parallel", "arbitrary", "arbitrary")),
        name="hyena",
    )(proj, proj, proj, cw, cw, cw, cb, cb, cb, we, wo, wie, wio, coef, skip, kmid)


def _cis_table(rows, cols, n):
    t = jnp.arange(cols, dtype=jnp.int32)
    digit = 16

    def cis(mult, count):
        j = jnp.arange(count, dtype=jnp.int32)
        ang = (2.0 * math.pi / n) * ((j[:, None] * mult * t[None, :]) % n).astype(F32)
        return jnp.cos(ang), jnp.sin(ang)

    def cmul(a, b):
        (ca, sa), (cb_, sb) = a, b
        c = ca[:, None] * cb_[None] - sa[:, None] * sb[None]
        s = sa[:, None] * cb_[None] + ca[:, None] * sb[None]
        return c.reshape(-1, cols), s.reshape(-1, cols)

    c, s = cmul(cis(digit * digit, -(-rows // (digit * digit))), cmul(cis(digit, digit), cis(1, digit)))
    return c[:rows], s[:rows]


def _dft_tables(l, fb):
    n, h = 2 * l, l // 2
    f = jnp.arange(h, dtype=F32)[:, None]
    ce, se = _cis_table(h, h, l)
    ct_, st_ = jnp.cos((2.0 * math.pi / n) * f), jnp.sin((2.0 * math.pi / n) * f)
    co, so = ce * ct_ - se * st_, se * ct_ + ce * st_
    cf = jnp.where(f == 0, 1.0 / n, 2.0 / n)

    def blocked(re, im):
        return jnp.concatenate([re.reshape(h // fb, 1, fb, h), im.reshape(h // fb, 1, fb, h)], axis=1).reshape(l, h)

    def inverse(c, s):
        return blocked(cf * c, -(2.0 / n) * s).reshape(h // fb, 2 * fb, h).transpose(0, 2, 1).astype(BF16)

    return (blocked(ce, -se).astype(BF16), blocked(co, -so).astype(BF16), inverse(ce, se), inverse(co, so))


def _implicit_filters(l, hp, parity):
    hi = lax.Precision.HIGHEST
    pos = jnp.arange(parity, l, 2, dtype=F32)
    t = pos / max(l - 1, 1)
    bands = jnp.linspace(1e-4, HY_BANDS - 1, HY_BANDS, dtype=F32)
    ang = (2.0 * math.pi / l) * pos[:, None] * bands[None, :]
    z = jnp.concatenate([t[:, None], jnp.cos(ang), -jnp.sin(ang)], axis=-1)
    freq = hp["hy_freq"]
    hdn = jnp.sin(freq[0] * (jnp.dot(z, hp["hy_w1"], precision=hi) + hp["hy_b1"]))
    hdn = jnp.sin(freq[1] * (jnp.dot(hdn, hp["hy_w2"], precision=hi) + hp["hy_b2"]))
    hdn = jnp.sin(freq[2] * (jnp.dot(hdn, hp["hy_w3"], precision=hi) + hp["hy_b3"]))
    d = hp["hy_decay"].shape[-1]
    h = jnp.dot(hdn, hp["hy_w_out"], precision=hi).reshape(pos.shape[0], 2, HY_ORDER, d)
    window = jnp.exp(-t[:, None, None] * jnp.abs(hp["hy_decay"]))
    return h * window[:, None]


def _filter_coefs(l, fb, ct, we, wo, hp):
    h = l // 2
    d = hp["hy_decay"].shape[-1]
    nct, nfb = d // ct, h // fb
    he = _implicit_filters(l, hp, 0).at[0, 1].set(0.0)
    ho = _implicit_filters(l, hp, 1)
    he2 = he.reshape(h, 2 * HY_ORDER * d)
    ho2 = ho.reshape(h, 2 * HY_ORDER * d)

    def body(we_ref, wo_ref, fe_ref, fo_ref, be_ref, bo_ref, o_ref):
        def spectrum(e_ref, o_ref_):
            a = jnp.dot(we_ref[...], e_ref[...], preferred_element_type=F32)
            b = jnp.dot(wo_ref[...], o_ref_[...], preferred_element_type=F32)
            return a[:fb] + b[:fb], a[fb:] + b[fb:], a[:fb] - b[:fb], b[fb:] - a[fb:]
        f_lre, f_lim, f_hre, f_him = spectrum(fe_ref, fo_ref)
        b_lre, b_lim, b_hre, b_him = spectrum(be_ref, bo_ref)
        o_ref[0, 0, 0, 0] = f_lre + b_lre
        o_ref[0, 0, 0, 1] = f_lim - b_lim
        o_ref[0, 0, 0, 2] = f_hre + b_hre
        o_ref[0, 0, 0, 3] = f_him - b_him

    def col(direction):
        return lambda o, j, k: (0, (direction * HY_ORDER + o) * nct + j)

    wspec = pl.BlockSpec((2 * fb, h), lambda o, j, k: (k, 0))
    coef = pl.pallas_call(
        body,
        grid=(HY_ORDER, nct, nfb),
        in_specs=[wspec, wspec, pl.BlockSpec((h, ct), col(0)), pl.BlockSpec((h, ct), col(0)),
                  pl.BlockSpec((h, ct), col(1)), pl.BlockSpec((h, ct), col(1))],
        out_specs=pl.BlockSpec((1, 1, 1, 4, fb, ct), lambda o, j, k: (o, j, k, 0, 0, 0)),
        out_shape=jax.ShapeDtypeStruct((HY_ORDER, nct, nfb, 4, fb, ct), F32),
        compiler_params=_cparams(("parallel", "parallel", "arbitrary")),
        name="filter_spectrum",
    )(we, wo, he2.astype(BF16), ho2.astype(BF16), he2.astype(BF16), ho2.astype(BF16))

    alt = jnp.where(jnp.arange(h) % 2 == 0, 1.0, -1.0).astype(F32)
    hi = lax.Precision.HIGHEST
    fre = jnp.dot(alt, he2, precision=hi).reshape(2, HY_ORDER, d)
    fim = -jnp.dot(alt, ho2, precision=hi).reshape(2, HY_ORDER, d)
    kmid = jnp.stack([fre[0] + fre[1], fim[0] - fim[1]], axis=1).reshape(2 * HY_ORDER, d)
    return coef, kmid


def _outproj_body(zhy_ref, yn_ref, ghy_ref, gssd_ref, x_ref, g1_ref, sh2_ref, sc2_ref, n2g_ref,
                  why_ref, wssm_ref, wo_ref, x1_ref, h2_ref, h2p_ref):
    hyo = jnp.dot(zhy_ref[...], why_ref[...], preferred_element_type=F32)
    sso = jnp.dot(yn_ref[...], wssm_ref[...], preferred_element_type=F32)
    merged = (jax.nn.sigmoid(ghy_ref[...].astype(F32)) * hyo
              + jax.nn.sigmoid(gssd_ref[...].astype(F32)) * sso)
    xmix = jnp.dot(merged.astype(BF16), wo_ref[...], preferred_element_type=F32)
    x1 = x_ref[0] + g1_ref[0] * xmix
    x1_ref[...] = x1
    r = lax.rsqrt(jnp.mean(x1 * x1, axis=-1, keepdims=True) + NORM_EPS)
    h2 = (x1 * r * n2g_ref[...]) * (1.0 + sc2_ref[0]) + sh2_ref[0]
    h2_ref[...] = h2
    h2p_ref[...] = _pack_bf16_pairs(h2)


def _outproj(zhy, yn, proj, x, gate1, shift2, scale2, n2g, why, wssm, wo, ghy_col, gssd_col):
    b, l, d = x.shape
    tm = min(l, 512)
    nl = l // tm
    tile = lambda col: pl.BlockSpec((tm, d), lambda i, j: (i * nl + j, col))
    row = pl.BlockSpec((1, 1, d), lambda i, j: (i, 0, 0))
    wspec = pl.BlockSpec((d, d), lambda i, j: (0, 0))
    return pl.pallas_call(
        _outproj_body,
        grid=(b, nl),
        in_specs=[tile(0), tile(0), tile(ghy_col), tile(gssd_col),
                  pl.BlockSpec((1, tm, d), lambda i, j: (i, j, 0)), row, row, row,
                  pl.BlockSpec((1, d), lambda i, j: (0, 0)), wspec, wspec, wspec],
        out_specs=[tile(0), tile(0), pl.BlockSpec((tm, d // 2), lambda i, j: (i * nl + j, 0))],
        out_shape=[jax.ShapeDtypeStruct((b * l, d), F32), jax.ShapeDtypeStruct((b * l, d), F32),
                   jax.ShapeDtypeStruct((b * l, d // 2), jnp.uint32)],
        compiler_params=_cparams(("parallel", "parallel")),
        name="merge_outproj",
    )(zhy, yn, proj, proj, x, gate1.reshape(b, 1, d), shift2.reshape(b, 1, d), scale2.reshape(b, 1, d),
      n2g.reshape(1, d), why, wssm, wo)


PER_GROUP = N_EXPERTS // N_EXPERT_GROUPS


def _route_body(tm, h_ref, wh_ref, wl_ref, bias_ref, e_ref, c_ref, w_ref, cnt_ref, carry_ref):
    ng, pg = N_EXPERT_GROUPS, PER_GROUP

    @pl.when(pl.program_id(0) == 0)
    def _():
        carry_ref[...] = jnp.zeros_like(carry_ref)

    x_hi, x_lo = _split_bf16(h_ref[...])
    nt = (((1,), (1,)), ((), ()))
    lg = (lax.dot_general(wh_ref[...], x_hi, nt, preferred_element_type=F32)
          + lax.dot_general(wh_ref[...], x_lo, nt, preferred_element_type=F32)
          + lax.dot_general(wl_ref[...], x_hi, nt, preferred_element_type=F32))
    sc = jax.nn.sigmoid(lg)
    sel = sc + bias_ref[...]

    mem = [sel[m * ng:(m + 1) * ng, :] for m in range(pg)]
    m1 = functools.reduce(jnp.maximum, mem)
    first = functools.reduce(jnp.minimum, [jnp.where(mem[m] == m1, m, pg) for m in range(pg)])
    m2 = functools.reduce(jnp.maximum, [jnp.where(first == m, -jnp.inf, mem[m]) for m in range(pg)])
    gs = m1 + m2
    gid = lax.broadcasted_iota(jnp.int32, gs.shape, 0)
    beats = jnp.zeros(gs.shape, jnp.int32)
    for g2 in range(ng):
        v = gs[g2:g2 + 1, :]
        beats = beats + ((v > gs) | ((v == gs) & (g2 < gid))).astype(jnp.int32)
    keep = beats < TOPK_GROUPS
    selm = jnp.concatenate([jnp.where(keep, mem[m], -jnp.inf) for m in range(pg)], axis=0)

    r = lax.broadcasted_iota(jnp.int32, selm.shape, 0)
    eid = (r % ng) * pg + r // ng
    rank = jnp.zeros(selm.shape, jnp.int32)
    for r2 in range(N_EXPERTS):
        v = selm[r2:r2 + 1, :]
        e2 = (r2 % ng) * pg + r2 // ng
        rank = rank + ((v > selm) | ((v == selm) & (e2 < eid))).astype(jnp.int32)
    chosen = rank < TOP_K
    wsel = jnp.where(chosen, sc, 0.0)
    wn = wsel / jnp.sum(wsel, axis=0, keepdims=True) * ROUTED_SCALE

    cb = chosen.astype(BF16)
    t0 = lax.broadcasted_iota(jnp.int32, (tm, tm), 0)
    t1 = lax.broadcasted_iota(jnp.int32, (tm, tm), 1)
    before = jnp.dot(cb, (t0 < t1).astype(BF16), preferred_element_type=F32) + carry_ref[...]
    carry = carry_ref[...] + jnp.dot(cb, jnp.ones((tm, tm), BF16), preferred_element_type=F32)
    carry_ref[...] = carry
    cnt_ref[...] = carry

    k8 = lax.broadcasted_iota(jnp.int32, e_ref.shape, 0)
    eidf = eid.astype(F32)
    e_out = jnp.zeros(e_ref.shape, F32)
    c_out = jnp.zeros(e_ref.shape, F32)
    w_out = jnp.zeros(e_ref.shape, F32)
    for k in range(TOP_K):
        mk = rank == k
        e_out = jnp.where(k8 == k, jnp.sum(jnp.where(mk, eidf, 0.0), axis=0, keepdims=True), e_out)
        c_out = jnp.where(k8 == k, jnp.sum(jnp.where(mk, before, 0.0), axis=0, keepdims=True), c_out)
        w_out = jnp.where(k8 == k, jnp.sum(jnp.where(mk, wn, 0.0), axis=0, keepdims=True), w_out)
    e_ref[...] = e_out.astype(jnp.int32)
    c_ref[...] = c_out.astype(jnp.int32)
    w_ref[...] = w_out


def _route(h2, router_w, router_bias):
    t, d = h2.shape
    tm = min(t, 256)
    ng, pg = N_EXPERT_GROUPS, PER_GROUP
    wt = router_w.T.reshape(ng, pg, d).transpose(1, 0, 2).reshape(N_EXPERTS, d)
    w_hi, w_lo = _split_bf16(wt)
    bias = jnp.broadcast_to(router_bias.astype(F32).reshape(ng, pg).T.reshape(N_EXPERTS, 1), (N_EXPERTS, tm))
    slot = pl.BlockSpec((8, tm), lambda i: (0, i))
    const = lambda shape: pl.BlockSpec(shape, lambda i: (0, 0))
    e8, c8, w8, cnt = pl.pallas_call(
        functools.partial(_route_body, tm),
        grid=(t // tm,),
        in_specs=[pl.BlockSpec((tm, d), lambda i: (i, 0)), const((N_EXPERTS, d)), const((N_EXPERTS, d)),
                  const((N_EXPERTS, tm))],
        out_specs=[slot, slot, slot, const((N_EXPERTS, tm))],
        out_shape=[jax.ShapeDtypeStruct((8, t), jnp.int32), jax.ShapeDtypeStruct((8, t), jnp.int32),
                   jax.ShapeDtypeStruct((8, t), F32), jax.ShapeDtypeStruct((N_EXPERTS, tm), F32)],
        scratch_shapes=[pltpu.VMEM((N_EXPERTS, tm), F32)],
        compiler_params=_cparams(("arbitrary",)),
        name="moe_route",
    )(h2, w_hi, w_lo, bias)
    counts = cnt[:, 0].reshape(pg, ng).T.reshape(N_EXPERTS).astype(jnp.int32)
    return e8[:TOP_K], c8[:TOP_K], w8[:TOP_K], counts


def _dispatch_plan(e6, c6, counts, n_slots):
    padded = (counts + EXPERT_BLOCK - 1) // EXPERT_BLOCK * EXPERT_BLOCK
    pad_end = jnp.cumsum(padded)
    pad_start = pad_end - padded
    nb = -(-n_slots // EXPERT_BLOCK) + N_EXPERTS
    blk = jnp.arange(nb, dtype=jnp.int32) * EXPERT_BLOCK
    block_e = jnp.minimum(jnp.sum((pad_end[None, :] <= blk[:, None]).astype(jnp.int32), axis=1), N_EXPERTS - 1)
    n_used = (pad_end[-1] // EXPERT_BLOCK).astype(jnp.int32).reshape(1)
    onehot = e6[..., None] == jnp.arange(N_EXPERTS, dtype=jnp.int32)
    pos = jnp.sum(jnp.where(onehot, pad_start, 0), axis=-1) + c6
    return pos.astype(jnp.int32), block_e.astype(jnp.int32), n_used, (pad_start + counts).astype(jnp.int32), \
        (padded - counts).astype(jnp.int32), nb


def _dispatch_body(tm, nt, nb, zs_ref, zn_ref, nu_ref, pos_ref, h_hbm, xs_hbm, buf_ref, zero_ref,
                   lsem_ref, sem_ref, zsem_ref):
    i = pl.program_id(0)
    slot = lax.rem(i, 3)
    nrow = TOP_K * tm

    def load(j, s):
        return pltpu.make_async_copy(h_hbm.at[pl.ds(pl.multiple_of(j * tm, tm), tm)], buf_ref.at[s], lsem_ref.at[s])

    @pl.when(i == 0)
    def _():
        load(0, 0).start()
        if nt > 1:
            load(1, 1).start()

    load(i, slot).wait()

    def body(r, carry):
        for k in range(TOP_K):
            pltpu.make_async_copy(buf_ref.at[slot, pl.ds(r, 1)],
                                  xs_hbm.at[pl.ds(pos_ref[0, 0, k * tm + r], 1)],
                                  sem_ref.at[slot]).start(priority=k % 2)
        return carry
    lax.fori_loop(0, tm, body, 0, unroll=8)

    def wait_batch(s):
        pltpu.make_async_copy(h_hbm.at[pl.ds(0, nrow)], xs_hbm.at[pl.ds(0, nrow)], sem_ref.at[s]).wait()

    nxt = lax.rem(i + 2, 3)

    @pl.when(i > 0)
    def _():
        wait_batch(nxt)

    @pl.when(i + 2 < nt)
    def _():
        load(i + 2, nxt).start()

    @pl.when(i == nt - 1)
    def _():
        wait_batch(slot)
        zero_ref[...] = jnp.zeros_like(zero_ref)

        def pad_copy(row):
            return pltpu.make_async_copy(zero_ref.at[pl.ds(0, 1)], xs_hbm.at[pl.ds(row, 1)], zsem_ref.at[0])

        def tail_copy(blk):
            row = pl.multiple_of(blk * EXPERT_BLOCK, EXPERT_BLOCK)
            return pltpu.make_async_copy(zero_ref, xs_hbm.at[pl.ds(row, EXPERT_BLOCK)], zsem_ref.at[1])

        def tail_start(blk, carry):
            tail_copy(blk).start()
            return carry
        lax.fori_loop(nu_ref[0], nb, tail_start, 0)

        def tail_wait(blk, carry):
            tail_copy(blk).wait()
            return carry
        lax.fori_loop(nu_ref[0], nb, tail_wait, 0)

        def pad_rows(e, wait):
            def body(j, carry):
                if wait:
                    pad_copy(zs_ref[e] + j).wait()
                else:
                    pad_copy(zs_ref[e] + j).start()
                return carry
            lax.fori_loop(0, zn_ref[e], body, 0)

        pad_rows(0, False)

        def per_expert(e, carry):
            pad_rows(e, False)
            pad_rows(e - 1, True)
            return carry
        lax.fori_loop(1, N_EXPERTS, per_expert, 0)
        pad_rows(N_EXPERTS - 1, True)


def _dispatch(h2, pos_t, zero_start, zero_n, n_used, nb, tm):
    t, d = h2.shape
    grid_spec = pltpu.PrefetchScalarGridSpec(
        num_scalar_prefetch=3,
        grid=(t // tm,),
        in_specs=[pl.BlockSpec((1, 1, TOP_K * tm), lambda i, zs, zn, nu: (i, 0, 0), memory_space=pltpu.SMEM),
                  pl.BlockSpec(memory_space=pl.ANY)],
        out_specs=pl.BlockSpec(memory_space=pl.ANY),
        scratch_shapes=[pltpu.VMEM((3, tm, d), h2.dtype), pltpu.VMEM((EXPERT_BLOCK, d), h2.dtype),
                        pltpu.SemaphoreType.DMA((3,)), pltpu.SemaphoreType.DMA((3,)),
                        pltpu.SemaphoreType.DMA((2,))],
    )
    return pl.pallas_call(
        functools.partial(_dispatch_body, tm, t // tm, nb),
        grid_spec=grid_spec,
        out_shape=jax.ShapeDtypeStruct((nb * EXPERT_BLOCK, d), h2.dtype),
        compiler_params=_cparams(("arbitrary",)),
        name="moe_dispatch",
    )(zero_start, zero_n, n_used, pos_t, h2)


def _experts_body(be_ref, nu_ref, x_ref, wg_ref, wu_ref, wd_ref, o_ref):
    used = pl.program_id(0) < nu_ref[0]

    @pl.when(used)
    def _():
        half = x_ref.shape[1]
        lo, hi = _unpack_bf16_pairs(x_ref[...])
        x_lo, x_hi = lo.astype(BF16), hi.astype(BF16)
        hg = (jnp.dot(x_lo, wg_ref[0, :half, :], preferred_element_type=F32)
              + jnp.dot(x_hi, wg_ref[0, half:, :], preferred_element_type=F32))
        hu = (jnp.dot(x_lo, wu_ref[0, :half, :], preferred_element_type=F32)
              + jnp.dot(x_hi, wu_ref[0, half:, :], preferred_element_type=F32))
        act = (_silu(hg) * hu).astype(BF16)
        o_ref[...] = _pack_bf16_pairs(jnp.dot(act, wd_ref[0], preferred_element_type=F32))

    @pl.when(jnp.logical_not(used))
    def _():
        o_ref[...] = jnp.zeros_like(o_ref)


def _experts(xs, block_e, n_used, wg, wu, wd):
    n_rows, dh = xs.shape
    d, de = wg.shape[1], wg.shape[2]
    rb = EXPERT_BLOCK
    last = lambda i, nu: jnp.minimum(i, nu[0] - 1)
    grid_spec = pltpu.PrefetchScalarGridSpec(
        num_scalar_prefetch=2,
        grid=(n_rows // rb,),
        in_specs=[pl.BlockSpec((rb, dh), lambda i, be, nu: (last(i, nu), 0)),
                  pl.BlockSpec((1, d, de), lambda i, be, nu: (be[last(i, nu)], 0, 0)),
                  pl.BlockSpec((1, d, de), lambda i, be, nu: (be[last(i, nu)], 0, 0)),
                  pl.BlockSpec((1, de, d), lambda i, be, nu: (be[last(i, nu)], 0, 0))],
        out_specs=pl.BlockSpec((rb, dh), lambda i, be, nu: (i, 0)),
    )
    return pl.pallas_call(
        _experts_body,
        grid_spec=grid_spec,
        out_shape=jax.ShapeDtypeStruct((n_rows, dh), jnp.uint32),
        compiler_params=_cparams(("arbitrary",)),
        name="experts",
    )(block_e, n_used, xs, wg, wu, wd)


def _combine_body(tm, pos_ref, posn_ref, ys_hbm, w_ref, h_ref, x1_ref, g2_ref, fg_ref,
                  sg_ref, su_ref, sd_ref, o_ref, buf_ref, sem_ref):
    i = pl.program_id(0)
    nt = pl.num_programs(0)
    slot = lax.rem(i, 2)
    nrow = TOP_K * tm

    def row_copy(src, r, s):
        return pltpu.make_async_copy(ys_hbm.at[pl.ds(src, 1)], buf_ref.at[s, pl.ds(r, 1)], sem_ref.at[s])

    def issue(idx_ref, s):
        def body(r, carry):
            for k in range(TOP_K):
                row_copy(idx_ref[0, 0, k * tm + r], k * tm + r, s).start(priority=k % 2)
            return carry
        lax.fori_loop(0, tm, body, 0, unroll=4)

    @pl.when(i == 0)
    def _():
        issue(pos_ref, 0)

    @pl.when(i + 1 < nt)
    def _():
        issue(posn_ref, 1 - slot)

    pltpu.make_async_copy(ys_hbm.at[pl.ds(0, nrow)], buf_ref.at[slot], sem_ref.at[slot]).wait()

    w = w_ref[...]
    half = buf_ref.shape[2]
    r_lo = jnp.zeros((tm, half), F32)
    r_hi = jnp.zeros((tm, half), F32)
    for k in range(TOP_K):
        lo, hi = _unpack_bf16_pairs(buf_ref[slot, k * tm:(k + 1) * tm, :])
        r_lo = r_lo + w[:, k:k + 1] * lo
        r_hi = r_hi + w[:, k:k + 1] * hi
    routed = jnp.concatenate([r_lo, r_hi], axis=1)
    xb = h_ref[...].astype(BF16)
    hg = jnp.dot(xb, sg_ref[...], preferred_element_type=F32)
    hu = jnp.dot(xb, su_ref[...], preferred_element_type=F32)
    shared = jnp.dot((_silu(hg) * hu).astype(BF16), sd_ref[...], preferred_element_type=F32)
    x2 = x1_ref[...] + g2_ref[0] * (routed + shared)
    r = lax.rsqrt(jnp.mean(x2 * x2, axis=-1, keepdims=True) + NORM_EPS)
    o_ref[0] = x2 * r * fg_ref[...]


def _tile_slots(pos, tm):
    k, t = pos.shape
    return pos.reshape(k, t // tm, tm).transpose(1, 0, 2).reshape(t // tm, 1, k * tm)


def _combine(ys, pos_t, w6, h2, x1, gate2, final_g, sg, su, sd, tokens_per_batch, tm):
    t, d = h2.shape
    nt = t // tm
    tpb = tokens_per_batch // tm
    ds = sg.shape[1]
    wpad = jnp.zeros((t, LANES), F32).at[:, :TOP_K].set(w6.T)
    tile = pl.BlockSpec((tm, d), lambda i: (i, 0))
    return pl.pallas_call(
        functools.partial(_combine_body, tm),
        grid=(nt,),
        in_specs=[pl.BlockSpec((1, 1, TOP_K * tm), lambda i: (i, 0, 0), memory_space=pltpu.SMEM),
                  pl.BlockSpec((1, 1, TOP_K * tm), lambda i: (jnp.minimum(i + 1, nt - 1), 0, 0),
                               memory_space=pltpu.SMEM),
                  pl.BlockSpec(memory_space=pl.ANY),
                  pl.BlockSpec((tm, LANES), lambda i: (i, 0)),
                  tile, tile,
                  pl.BlockSpec((1, 1, d), lambda i: (i // tpb, 0, 0)),
                  pl.BlockSpec((1, d), lambda i: (0, 0)),
                  pl.BlockSpec((d, ds), lambda i: (0, 0)),
                  pl.BlockSpec((d, ds), lambda i: (0, 0)),
                  pl.BlockSpec((ds, d), lambda i: (0, 0))],
        out_specs=pl.BlockSpec((1, tm, d), lambda i: (i // tpb, i % tpb, 0)),
        out_shape=jax.ShapeDtypeStruct((t // tokens_per_batch, tokens_per_batch, d), F32),
        scratch_shapes=[pltpu.VMEM((2, TOP_K * tm, d // 2), jnp.uint32), pltpu.SemaphoreType.DMA((2,))],
        compiler_params=_cparams(("arbitrary",)),
        name="moe_combine",
    )(pos_t, pos_t, ys, wpad, h2, x1, gate2, final_g.reshape(1, d), sg, su, sd)


def _ssd_params(ssm_conv_w, ssm_conv_b, ssm_dt_bias, ssm_a_log, ssm_d, ssm_norm_g):
    dx = SSM_HEADS * SSM_HEAD_DIM
    q = SSD_CHUNK
    pad = lambda v: jnp.zeros((2, 1, LANES), F32).at[:, 0, :SSM_HEADS].set(v)
    heads = jnp.arange(SSM_HEADS)
    e1 = (jnp.arange(dx)[None, :] // SSM_HEAD_DIM == heads[:, None]).astype(F32)
    e2 = (jnp.arange(SSM_HEADS * q)[None, :] // q == heads[:, None]).astype(F32)

    def stack(e):
        z = jnp.zeros((2 * LANES, e.shape[1]), F32)
        return z.at[:SSM_HEADS].set(e).at[LANES:LANES + SSM_HEADS].set(e).astype(BF16)

    return dict(
        cwx=ssm_conv_w[:, :dx], cbx=ssm_conv_b[None, :dx],
        cwbc=ssm_conv_w[:, dx:], cbbc=ssm_conv_b[None, dx:],
        dtb=pad(ssm_dt_bias), a=pad(-jnp.exp(ssm_a_log.astype(F32))),
        dexp=jnp.repeat(ssm_d, SSM_HEAD_DIM)[None, :], ng=ssm_norm_g[None, :],
        e=stack(e1), e2=stack(e2))


def kernel(x, c, ctx, c_ctx, ada_w, ada_b, norm1_g, norm2_g, w_in, ssm_conv_w, ssm_conv_b, ssm_dt_bias, ssm_a_log, ssm_d, ssm_norm_g, w_ssm_out, hy_conv_w, hy_conv_b, hy_w1, hy_b1, hy_w2, hy_b2, hy_w3, hy_b3, hy_freq, hy_w_out, hy_decay, hy_bias, w_hy_out, w_o, router_w, router_bias, e_w_gate, e_w_up, e_w_down, sh_w_gate, sh_w_up, sh_w_down, final_g):
    depth = ada_w.shape[0]
    assert depth == 1, "single-layer block"
    b, l, d = x.shape
    lc = ctx.shape[1]
    dx = SSM_HEADS * SSM_HEAD_DIM
    dbc = 2 * SSM_GROUPS * SSM_STATE
    assert dx == d and hy_bias.shape[-1] == d

    rows = -(-(b + 1) // BF16_SUBLANES) * BF16_SUBLANES
    cc = jnp.zeros((rows, d), F32).at[:b].set(c).at[b].set(c_ctx)
    mod_all = _matmul(_silu(cc).astype(BF16), ada_w[0].astype(BF16), F32, rows, 512, "adaln") + ada_b[0]
    mod = mod_all[:b].reshape(b, 6, d)
    mod_c = jnp.broadcast_to(mod_all[b].reshape(1, 6, d), (b, 6, d))

    w = w_in[0]
    o_dt, o_z, o_hy, o_g = dx + dbc, dx + dbc + 2 * SSM_HEADS, 2 * dx + dbc + 2 * SSM_HEADS, 2 * dx + dbc + 2 * SSM_HEADS + 3 * d
    w_main = jnp.concatenate([w[:, o_hy:o_g], w[:, o_g:], w[:, o_z:o_hy], w[:, :dx + dbc]], axis=1).astype(BF16)
    w_dt = (jnp.zeros((d, 2 * LANES), F32)
            .at[:, :SSM_HEADS].set(w[:, o_dt:o_dt + SSM_HEADS])
            .at[:, LANES:LANES + SSM_HEADS].set(w[:, o_dt + SSM_HEADS:o_z])).astype(BF16)
    col_ghy, col_gssd, col_z, col_xs, col_bc = 3, 4, 5, 6, 7 * (d // dbc)

    sp = _ssd_params(ssm_conv_w[0], ssm_conv_b[0], ssm_dt_bias[0], ssm_a_log[0], ssm_d[0], ssm_norm_g[0])

    w_ctx = w_main[:, col_xs * d:]
    proj_c, dt_c = _norm_proj(ctx, norm1_g[0], mod_c[:, 0], mod_c[:, 1], w_ctx, w_dt, 1024, 512, "ctx_proj")
    zero_state = jnp.zeros((b, 2, SSM_STATE, dx), F32)
    ctx_state = _ssd(proj_c, dt_c, zero_state, sp, False, 0, d // dbc, 0)

    proj, dt_raw = _norm_proj(x, norm1_g[0], mod[:, 0], mod[:, 1], w_main, w_dt, 1024, 1536, "in_proj")
    yn = _ssd(proj, dt_raw, ctx_state, sp, True, col_xs, col_bc, col_z)

    fb = min(l // 2, 256)
    ct = 512
    tables = _dft_tables(l, fb)
    hp = dict(hy_w1=hy_w1[0], hy_b1=hy_b1[0], hy_w2=hy_w2[0], hy_b2=hy_b2[0], hy_w3=hy_w3[0], hy_b3=hy_b3[0],
              hy_freq=hy_freq[0], hy_w_out=hy_w_out[0], hy_decay=hy_decay[0])
    coef, kmid = _filter_coefs(l, fb, ct, tables[0], tables[1], hp)
    zhy = _hyena(proj, b, hy_conv_w[0], hy_conv_b[0][None, :], tables, coef, kmid, hy_bias[0], ct, fb)

    x1, h2, h2p = _outproj(zhy, yn, proj, x, mod[:, 2], mod[:, 3], mod[:, 4], norm2_g[0],
                      w_hy_out[0].astype(BF16), w_ssm_out[0].astype(BF16), w_o[0].astype(BF16),
                      col_ghy, col_gssd)

    e6, c6, w6, counts = _route(h2, router_w[0], router_bias[0])
    pos, block_e, n_used, zero_start, zero_n, nb = _dispatch_plan(e6, c6, counts, b * l * TOP_K)
    tm = min(l, 128)
    pos_t = _tile_slots(pos, tm)
    xs = _dispatch(h2p, pos_t, zero_start, zero_n, n_used, nb, tm)
    ys = _experts(xs, block_e, n_used, e_w_gate[0].astype(BF16), e_w_up[0].astype(BF16),
                  e_w_down[0].astype(BF16))
    return _combine(ys, pos_t, w6, h2, x1, mod[:, 5].reshape(b, 1, d), final_g,
                    sh_w_gate[0].astype(BF16), sh_w_up[0].astype(BF16), sh_w_down[0].astype(BF16), l, tm)
```
